```python
import jax, jax.numpy as jnp
from jax import lax
import numpy as np

D_MODEL = 1024
BATCH = 4
SEQ = 8192
DEPTH = 1

CHUNK = 64
N_PREV_CHUNKS = 8
BAND = (N_PREV_CHUNKS + 1) * CHUNK
ATT_HEADS = 8
ATT_HEAD_DIM = 64
ATT_WIDTH = ATT_HEADS * ATT_HEAD_DIM
REL_CLIP = 128
SG_BLOCK = 128
SG_GROUPS = 8
SG_GROUP_DIM = 64
SG_WIDTH = SG_GROUPS * SG_GROUP_DIM
N_BRANCHES = 2
IN_COLS = 3 * ATT_WIDTH + 2 * SG_WIDTH + N_BRANCHES * D_MODEL
MEM_LEN = 256
XATT_HEADS = 4
XATT_HEAD_DIM = D_MODEL // XATT_HEADS
D_FF = -(-8 * D_MODEL // (3 * 256)) * 256
EPS = 1e-6
NEG_INF = -1e30

kernel_name = "hybrid_chunk_attn_sgu_block"


def rmsnorm(x, g):
    xf = x.astype(jnp.float32)
    y = xf * lax.rsqrt(jnp.mean(xf * xf, axis=-1, keepdims=True) + EPS)
    return (y * g.astype(jnp.float32)).astype(x.dtype)


def layernorm(x, g, b):
    xf = x.astype(jnp.float32)
    mu = jnp.mean(xf, axis=-1, keepdims=True)
    var = jnp.mean(jnp.square(xf - mu), axis=-1, keepdims=True)
    y = (xf - mu) * lax.rsqrt(var + EPS)
    return (y * g.astype(jnp.float32) + b.astype(jnp.float32)).astype(x.dtype)


def chunked_relpos_attention(q, k, v, rel_bias):
    B, S, H, Dh = q.shape
    nC = S // CHUNK
    q = (q * (Dh ** -0.5)).reshape(B, nC, CHUNK, H, Dh)
    k = k.reshape(B, nC, CHUNK, H, Dh)
    v = v.reshape(B, nC, CHUNK, H, Dh)
    pad = ((0, 0), (N_PREV_CHUNKS, 0), (0, 0), (0, 0), (0, 0))
    kp = jnp.pad(k, pad)
    vp = jnp.pad(v, pad)
    kb = jnp.stack([kp[:, j:j + nC] for j in range(N_PREV_CHUNKS + 1)], axis=2).reshape(B, nC, BAND, H, Dh)
    vb = jnp.stack([vp[:, j:j + nC] for j in range(N_PREV_CHUNKS + 1)], axis=2).reshape(B, nC, BAND, H, Dh)
    s = jnp.einsum('bcihd,bcmhd->bhcim', q, kb, preferred_element_type=jnp.float32)
    qi = np.arange(CHUNK)[:, None]
    mi = np.arange(BAND)[None, :]
    dist = qi - mi + N_PREV_CHUNKS * CHUNK
    idx = np.clip(dist, -REL_CLIP, REL_CLIP) + REL_CLIP
    bias = rel_bias[:, idx].astype(jnp.float32)
    s = s + bias[None, :, None]
    valid = (np.arange(nC)[:, None] - N_PREV_CHUNKS + np.arange(BAND)[None, :] // CHUNK) >= 0
    s = jnp.where(valid[None, None, :, None, :], s, NEG_INF)
    p = jax.nn.softmax(s, axis=-1)
    o = jnp.einsum('bhcim,bcmhd->bcihd', p.astype(vb.dtype), vb)
    return o.reshape(B, S, H * Dh)


def spatial_gating(u, v, ln_g, ln_b, w_s, b_s):
    B, S, _ = u.shape
    nB = S // SG_BLOCK
    v = v.reshape(B, nB, SG_BLOCK, SG_GROUPS, SG_GROUP_DIM)
    v = layernorm(v, ln_g, ln_b)
    t = np.arange(SG_BLOCK)
    mask = (t[None, :] // CHUNK) <= (t[:, None] // CHUNK)
    w = jnp.where(mask[None], w_s, 0.0)
    sv = jnp.einsum('gts,bnsgd->bntgd', w, v) + b_s.T[None, None, :, :, None]
    return (u.reshape(B, nB, SG_BLOCK, SG_GROUPS, SG_GROUP_DIM) * sv).reshape(B, S, SG_WIDTH)


def cross_attention(h, m, w_xq, w_xkv, w_xo):
    B, S, _ = h.shape
    M = m.shape[1]
    q = (h @ w_xq).reshape(B, S, XATT_HEADS, XATT_HEAD_DIM) * (XATT_HEAD_DIM ** -0.5)
    k, v = jnp.split(m @ w_xkv, 2, axis=-1)
    k = k.reshape(B, M, XATT_HEADS, XATT_HEAD_DIM)
    v = v.reshape(B, M, XATT_HEADS, XATT_HEAD_DIM)
    s = jnp.einsum('bshd,bmhd->bhsm', q, k, preferred_element_type=jnp.float32)
    p = jax.nn.softmax(s, axis=-1)
    o = jnp.einsum('bhsm,bmhd->bshd', p.astype(v.dtype), v).reshape(B, S, D_MODEL)
    return o @ w_xo


def setup_inputs(seed: int = 0) -> dict:
    key = jax.random.key(seed)
    ks = jax.random.split(key, 24)
    f32 = jnp.float32
    nrm = lambda k, shape, scale: jax.random.normal(k, shape, f32) * scale
    L = DEPTH
    return {
        "x": nrm(ks[0], (BATCH, SEQ, D_MODEL), 1.0),
        "mem": nrm(ks[1], (BATCH, MEM_LEN, D_MODEL), 1.0),
        "norm_mix_g": 1.0 + nrm(ks[2], (L, D_MODEL), 0.02),
        "w_in": nrm(ks[3], (L, D_MODEL, IN_COLS), D_MODEL ** -0.5),
        "rel_bias": nrm(ks[4], (L, ATT_HEADS, 2 * REL_CLIP + 1), 0.5),
        "sg_ln_g": 1.0 + nrm(ks[5], (L, SG_GROUPS, SG_GROUP_DIM), 0.02),
        "sg_ln_b": nrm(ks[6], (L, SG_GROUPS, SG_GROUP_DIM), 0.02),
        "sg_w": nrm(ks[7], (L, SG_GROUPS, SG_BLOCK, SG_BLOCK), SG_BLOCK ** -0.5),
        "sg_b": 1.0 + nrm(ks[8], (L, SG_GROUPS, SG_BLOCK), 0.02),
        "w_branch_att": nrm(ks[9], (L, ATT_WIDTH, D_MODEL), ATT_WIDTH ** -0.5),
        "w_branch_sg": nrm(ks[10], (L, SG_WIDTH, D_MODEL), SG_WIDTH ** -0.5),
        "w_out": nrm(ks[11], (L, D_MODEL, D_MODEL), D_MODEL ** -0.5),
        "norm_xattn_g": 1.0 + nrm(ks[12], (L, D_MODEL), 0.02),
        "norm_mem_g": 1.0 + nrm(ks[13], (L, D_MODEL), 0.02),
        "w_xq": nrm(ks[14], (L, D_MODEL, D_MODEL), D_MODEL ** -0.5),
        "w_xkv": nrm(ks[15], (L, D_MODEL, 2 * D_MODEL), D_MODEL ** -0.5),
        "w_xo": nrm(ks[16], (L, D_MODEL, D_MODEL), D_MODEL ** -0.5),
        "norm_ffn_g": 1.0 + nrm(ks[17], (L, D_MODEL), 0.02),
        "w_ffn_in": nrm(ks[18], (L, D_MODEL, 2 * D_FF), D_MODEL ** -0.5),
        "w_ffn_out": nrm(ks[19], (L, D_FF, D_MODEL), D_FF ** -0.5),
        "norm_final_g": 1.0 + nrm(ks[20], (D_MODEL,), 0.02),
    }


def reference(x, mem, norm_mix_g, w_in, rel_bias, sg_ln_g, sg_ln_b, sg_w, sg_b,
              w_branch_att, w_branch_sg, w_out, norm_xattn_g, norm_mem_g,
              w_xq, w_xkv, w_xo, norm_ffn_g, w_ffn_in, w_ffn_out, norm_final_g):
    B, S, _ = x.shape
    col = np.cumsum([ATT_WIDTH, ATT_WIDTH, ATT_WIDTH, SG_WIDTH, SG_WIDTH, D_MODEL])
    for l in range(DEPTH):
        h = rmsnorm(x, norm_mix_g[l])
        z = h @ w_in[l]
        q, k, v, u_sg, v_sg, g_a, g_b = jnp.split(z, col, axis=-1)
        q = q.reshape(B, S, ATT_HEADS, ATT_HEAD_DIM)
        k = k.reshape(B, S, ATT_HEADS, ATT_HEAD_DIM)
        v = v.reshape(B, S, ATT_HEADS, ATT_HEAD_DIM)
        y_att = chunked_relpos_attention(q, k, v, rel_bias[l])
        y_sg = spatial_gating(jax.nn.gelu(u_sg), jax.nn.gelu(v_sg),
                              sg_ln_g[l], sg_ln_b[l], sg_w[l], sg_b[l])
        merged = (jax.nn.sigmoid(g_a) * (y_att @ w_branch_att[l])
                  + jax.nn.sigmoid(g_b) * (y_sg @ w_branch_sg[l]))
        x = x + merged @ w_out[l]
        x = x + cross_attention(rmsnorm(x, norm_xattn_g[l]), rmsnorm(mem, norm_mem_g[l]),
                                w_xq[l], w_xkv[l], w_xo[l])
        gate, up = jnp.split(rmsnorm(x, norm_ffn_g[l]) @ w_ffn_in[l], 2, axis=-1)
        x = x + (jax.nn.silu(gate) * up) @ w_ffn_out[l]
    return rmsnorm(x, norm_final_g)
```

```python
import functools

import jax
import jax.numpy as jnp
import numpy as np
from jax import lax
from jax.experimental import pallas as pl
from jax.experimental.pallas import tpu as pltpu

CHUNK = 64
N_PREV_CHUNKS = 8
ATT_HEADS = 8
ATT_HEAD_DIM = 64
ATT_WIDTH = ATT_HEADS * ATT_HEAD_DIM
REL_CLIP = 128
SG_BLOCK = 128
SG_GROUPS = 8
SG_GROUP_DIM = 64
SG_WIDTH = SG_GROUPS * SG_GROUP_DIM
XATT_HEADS = 4
EPS = 1e-6
NEG_INF = -1e30

LANES = 128
VMEM_LIMIT_BYTES = 56 * 1024 * 1024

TOKEN_TILE = 512
Q_SUB = 2 * CHUNK
PREV_KEYS = N_PREV_CHUNKS * CHUNK
WINDOW = PREV_KEYS + Q_SUB
FF_CHUNK = 256

F32 = jnp.float32
BF16 = jnp.bfloat16
_NT = (((1,), (1,)), ((), ()))


def _dot(a, b):
    return jnp.dot(a, b, preferred_element_type=F32)


def _dot_nt(a, b):
    return lax.dot_general(a, b, _NT, preferred_element_type=F32)


def _rmsnorm(x, g):
    return x * lax.rsqrt(jnp.mean(x * x, axis=-1, keepdims=True) + EPS) * g


def _const_spec(shape):
    zeros = (0,) * len(shape)
    return pl.BlockSpec(shape, lambda *_: zeros, pipeline_mode=pl.Buffered(1))


def _params():
    return pltpu.CompilerParams(dimension_semantics=("arbitrary",),
                                vmem_limit_bytes=VMEM_LIMIT_BYTES)


def _bias_kernel(rb_ref, o_ref):
    var_cols = WINDOW - (PREV_KEYS - REL_CLIP)
    col0 = WINDOW - var_cols
    qi = lax.broadcasted_iota(jnp.int32, (Q_SUB, var_cols), 0)
    km = lax.broadcasted_iota(jnp.int32, (Q_SUB, var_cols), 1) + col0
    idx = jnp.clip(PREV_KEYS + qi - km, -REL_CLIP, REL_CLIP) + REL_CLIP
    qc = lax.broadcasted_iota(jnp.int32, (Q_SUB, WINDOW), 0) // CHUNK
    kc = lax.broadcasted_iota(jnp.int32, (Q_SUB, WINDOW), 1) // CHUNK
    valid = (kc >= qc) & (kc <= qc + N_PREV_CHUNKS)
    for h in range(ATT_HEADS):
        def body(j, t, h=h):
            return jnp.where(idx == j, rb_ref[h, j], t)
        t = lax.fori_loop(0, 2 * REL_CLIP + 1, body, jnp.zeros((Q_SUB, var_cols), F32))
        far = jnp.full((Q_SUB, col0), rb_ref[h, 2 * REL_CLIP], F32)
        o_ref[h] = jnp.where(valid, jnp.concatenate([far, t], axis=1), NEG_INF)


def _bias_table(rel_bias):
    return pl.pallas_call(
        _bias_kernel,
        out_shape=jax.ShapeDtypeStruct((ATT_HEADS, Q_SUB, WINDOW), F32),
        in_specs=[pl.BlockSpec(memory_space=pltpu.SMEM)],
        out_specs=pl.BlockSpec(memory_space=pltpu.VMEM),
        name="bias_table",
    )(rel_bias)


def _memkv_kernel(mem_ref, g_ref, wkT_ref, wv_ref, kT_ref, v_ref):
    mn = _rmsnorm(mem_ref[0], g_ref[...]).astype(BF16)
    kT_ref[0] = _dot_nt(wkT_ref[...], mn).astype(BF16)
    v_ref[0] = _dot(mn, wv_ref[...]).astype(BF16)


def _mem_kv(mem, g, wkT, wv):
    b, m, d = mem.shape
    return pl.pallas_call(
        _memkv_kernel,
        grid=(b,),
        out_shape=(jax.ShapeDtypeStruct((b, d, m), BF16), jax.ShapeDtypeStruct((b, m, d), BF16)),
        in_specs=[pl.BlockSpec((1, m, d), lambda i: (i, 0, 0)),
                  _const_spec((1, d)), _const_spec((d, d)), _const_spec((d, d))],
        out_specs=(pl.BlockSpec((1, d, m), lambda i: (i, 0, 0)),
                   pl.BlockSpec((1, m, d), lambda i: (i, 0, 0))),
        compiler_params=_params(),
        name="mem_kv",
    )(mem, g, wkT, wv)


def _inproj_kernel(x_ref, g_ref, wq_ref, wkT_ref, wv_ref, wu_ref, wvs_ref, wga_ref, wgb_ref,
                   avg_ref, lng_ref, lnb_ref, sgw_ref, sgb_ref, wbsg_ref,
                   q_ref, kT_ref, v_ref, ga_ref, mb_ref, wm_scr, ysg_scr):
    pair_rows = 2 * SG_BLOCK

    @pl.when(pl.program_id(0) == 0)
    def _():
        t = lax.broadcasted_iota(jnp.int32, (pair_rows, SG_BLOCK), 0) % SG_BLOCK
        s = lax.broadcasted_iota(jnp.int32, (pair_rows, SG_BLOCK), 1)
        mask = (s // CHUNK) <= (t // CHUNK)
        for j in range(SG_GROUPS // 2):
            wm_scr[j] = jnp.where(mask, sgw_ref[j], 0.0).astype(BF16)

    hb = _rmsnorm(x_ref[...], g_ref[...]).astype(BF16)
    q_ref[...] = (_dot(hb, wq_ref[...]) * (ATT_HEAD_DIM ** -0.5)).astype(BF16)
    kT_ref[...] = _dot_nt(wkT_ref[...], hb).astype(BF16)
    v_ref[...] = _dot(hb, wv_ref[...]).astype(BF16)

    u = jax.nn.gelu(_dot(hb, wu_ref[...]))
    vs = jax.nn.gelu(_dot(hb, wvs_ref[...]))
    dev = vs - _dot(vs.astype(BF16), avg_ref[...])
    var = _dot((dev * dev).astype(BF16), avg_ref[...])
    vln = (dev * lax.rsqrt(var + EPS) * lng_ref[...] + lnb_ref[...]).astype(BF16)

    lane = lax.broadcasted_iota(jnp.int32, (SG_BLOCK, LANES), 1)
    for n in range(x_ref.shape[0] // SG_BLOCK):
        rows = slice(n * SG_BLOCK, (n + 1) * SG_BLOCK)
        for j in range(SG_GROUPS // 2):
            cols = slice(j * LANES, (j + 1) * LANES)
            r = _dot(wm_scr[j], vln[rows, cols])
            sv = jnp.where(lane < SG_GROUP_DIM, r[:SG_BLOCK], r[SG_BLOCK:]) + sgb_ref[:, cols]
            ysg_scr[rows, cols] = (u[rows, cols] * sv).astype(BF16)

    gate_b = jax.nn.sigmoid(_dot(hb, wgb_ref[...]))
    mb_ref[...] = (gate_b * _dot(ysg_scr[...], wbsg_ref[...])).astype(BF16)
    ga_ref[...] = jax.nn.sigmoid(_dot(hb, wga_ref[...])).astype(BF16)


def _inproj(x, g, wq, wkT, wv, wu, wvs, wga, wgb, avg, lng, lnb, sgw, sgb, wbsg):
    n, d = x.shape
    tm = TOKEN_TILE
    tile = lambda w: pl.BlockSpec((tm, w), lambda i: (i, 0))
    return pl.pallas_call(
        _inproj_kernel,
        grid=(n // tm,),
        out_shape=(jax.ShapeDtypeStruct((n, ATT_WIDTH), BF16),
                   jax.ShapeDtypeStruct((ATT_WIDTH, n), BF16),
                   jax.ShapeDtypeStruct((n, ATT_WIDTH), BF16),
                   jax.ShapeDtypeStruct((n, d), BF16),
                   jax.ShapeDtypeStruct((n, d), BF16)),
        in_specs=[tile(d), _const_spec((1, d)),
                  _const_spec(wq.shape), _const_spec(wkT.shape), _const_spec(wv.shape),
                  _const_spec(wu.shape), _const_spec(wvs.shape), _const_spec(wga.shape),
                  _const_spec(wgb.shape), _const_spec(avg.shape), _const_spec(lng.shape),
                  _const_spec(lnb.shape), _const_spec(sgw.shape), _const_spec(sgb.shape),
                  _const_spec(wbsg.shape)],
        out_specs=(tile(ATT_WIDTH), pl.BlockSpec((ATT_WIDTH, tm), lambda i: (0, i)),
                   tile(ATT_WIDTH), tile(d), tile(d)),
        scratch_shapes=[pltpu.VMEM((SG_GROUPS // 2, 2 * SG_BLOCK, SG_BLOCK), BF16),
                        pltpu.VMEM((tm, SG_WIDTH), BF16)],
        compiler_params=_params(),
        name="inproj_sgu",
    )(x, g, wq, wkT, wv, wu, wvs, wga, wgb, avg, lng, lnb, sgw, sgb, wbsg)


def _attn_kernel(tiles_per_seq, x_ref, q_ref, kTp_ref, kTc_ref, vp_ref, vc_ref, ga_ref, mb_ref,
                 bias_ref, wbatt_ref, wout_ref, gx_ref, wxq_ref, kmT_ref, vm_ref, wxo_ref,
                 o_ref, kT_scr, v_scr, yatt_scr, xo_scr):
    tm = x_ref.shape[0]
    d = x_ref.shape[1]
    first_tile = (pl.program_id(0) % tiles_per_seq) == 0

    kT_scr[:, :tm] = kTp_ref[...]
    kT_scr[:, tm:] = kTc_ref[...]
    v_scr[:tm] = vp_ref[...]
    v_scr[tm:] = vc_ref[...]

    lane = lax.broadcasted_iota(jnp.int32, (Q_SUB, LANES), 1)
    col = lax.broadcasted_iota(jnp.int32, (1, WINDOW), 1)
    for j in range(tm // Q_SUB):
        q0 = j * Q_SUB
        n_before = jnp.where(first_tile, max(PREV_KEYS - q0, 0), 0)
        pen = jnp.where(col < n_before, NEG_INF, 0.0)
        w0 = tm + q0 - PREV_KEYS
        for p in range(ATT_HEADS // 2):
            cols = slice(p * LANES, (p + 1) * LANES)
            qp = q_ref[q0:q0 + Q_SUB, cols]
            kTw = kT_scr[cols, w0:w0 + WINDOW]
            vw = v_scr[w0:w0 + WINDOW, cols]
            outs = []
            for hh in range(2):
                qz = jnp.where((lane < ATT_HEAD_DIM) == (hh == 0), qp, jnp.zeros_like(qp))
                s = _dot(qz, kTw) + bias_ref[2 * p + hh] + pen
                e = jnp.exp(s - jnp.max(s, axis=-1, keepdims=True))
                den = jnp.sum(e, axis=-1, keepdims=True)
                outs.append(_dot(e.astype(BF16), vw) / den)
            y = jnp.where(lane < ATT_HEAD_DIM, outs[0], outs[1])
            yatt_scr[q0:q0 + Q_SUB, cols] = y.astype(BF16)

    att = _dot(yatt_scr[...], wbatt_ref[...])
    merged = ga_ref[...].astype(F32) * att + mb_ref[...].astype(F32)
    x1 = x_ref[...] + _dot(merged.astype(BF16), wout_ref[...])

    hx = _rmsnorm(x1, gx_ref[...]).astype(BF16)
    xd = d // XATT_HEADS
    qx = (_dot(hx, wxq_ref[...]) * (xd ** -0.5)).astype(BF16)
    for h in range(XATT_HEADS):
        cols = slice(h * xd, (h + 1) * xd)
        s = _dot(qx[:, cols], kmT_ref[0, cols, :])
        e = jnp.exp(s - jnp.max(s, axis=-1, keepdims=True))
        den = jnp.sum(e, axis=-1, keepdims=True)
        xo_scr[:, cols] = (_dot(e.astype(BF16), vm_ref[0, :, cols]) / den).astype(BF16)
    o_ref[...] = x1 + _dot(xo_scr[...], wxo_ref[...])


def _attn(x, q, kT, v, ga, mb, bias, wbatt, wout, gx, wxq, kmT, vm, wxo, seq):
    n, d = x.shape
    tm = TOKEN_TILE
    tps = seq // tm
    mem_len = vm.shape[1]
    tile = lambda w: pl.BlockSpec((tm, w), lambda i: (i, 0))
    prev = lambda i: jnp.maximum(i - 1, 0)
    return pl.pallas_call(
        functools.partial(_attn_kernel, tps),
        grid=(n // tm,),
        out_shape=jax.ShapeDtypeStruct((n, d), F32),
        in_specs=[tile(d), tile(ATT_WIDTH),
                  pl.BlockSpec((ATT_WIDTH, tm), lambda i: (0, prev(i))),
                  pl.BlockSpec((ATT_WIDTH, tm), lambda i: (0, i)),
                  pl.BlockSpec((tm, ATT_WIDTH), lambda i: (prev(i), 0)),
                  tile(ATT_WIDTH), tile(d), tile(d),
                  _const_spec(bias.shape), _const_spec(wbatt.shape), _const_spec(wout.shape),
                  _const_spec((1, d)), _const_spec(wxq.shape),
                  pl.BlockSpec((1, d, mem_len), lambda i: (i // tps, 0, 0)),
                  pl.BlockSpec((1, mem_len, d), lambda i: (i // tps, 0, 0)),
                  _const_spec(wxo.shape)],
        out_specs=tile(d),
        scratch_shapes=[pltpu.VMEM((ATT_WIDTH, 2 * tm), BF16),
                        pltpu.VMEM((2 * tm, ATT_WIDTH), BF16),
                        pltpu.VMEM((tm, ATT_WIDTH), BF16),
                        pltpu.VMEM((tm, d), BF16)],
        compiler_params=_params(),
        name="attn_merge_xattn",
    )(x, q, kT, kT, v, v, ga, mb, bias, wbatt, wout, gx, wxq, kmT, vm, wxo)


def _ffn_kernel(x_ref, g_ref, wfi_ref, wfo_ref, gfin_ref, o_ref, a_scr):
    d_ff = wfo_ref.shape[0]
    x = x_ref[...]
    hb = _rmsnorm(x, g_ref[...]).astype(BF16)
    for c in range(d_ff // FF_CHUNK):
        gate = _dot(hb, wfi_ref[:, c * FF_CHUNK:(c + 1) * FF_CHUNK])
        up = _dot(hb, wfi_ref[:, d_ff + c * FF_CHUNK:d_ff + (c + 1) * FF_CHUNK])
        a_scr[:, c * FF_CHUNK:(c + 1) * FF_CHUNK] = (jax.nn.silu(gate) * up).astype(BF16)
    x3 = x + _dot(a_scr[...], wfo_ref[...])
    o_ref[...] = _rmsnorm(x3, gfin_ref[...])


def _ffn(x, g, wfi, wfo, gfin):
    n, d = x.shape
    tm = TOKEN_TILE
    d_ff = wfo.shape[0]
    assert d_ff % FF_CHUNK == 0
    tile = pl.BlockSpec((tm, d), lambda i: (i, 0))
    return pl.pallas_call(
        _ffn_kernel,
        grid=(n // tm,),
        out_shape=jax.ShapeDtypeStruct((n, d), F32),
        in_specs=[tile, _const_spec((1, d)), _const_spec(wfi.shape), _const_spec(wfo.shape),
                  _const_spec((1, d))],
        out_specs=tile,
        scratch_shapes=[pltpu.VMEM((tm, d_ff), BF16)],
        compiler_params=_params(),
        name="ffn_final_norm",
    )(x, g, wfi, wfo, gfin)


def kernel(x, mem, norm_mix_g, w_in, rel_bias, sg_ln_g, sg_ln_b, sg_w, sg_b, w_branch_att,
           w_branch_sg, w_out, norm_xattn_g, norm_mem_g, w_xq, w_xkv, w_xo, norm_ffn_g,
           w_ffn_in, w_ffn_out, norm_final_g):
    b, s, d = x.shape
    depth = w_in.shape[0]
    assert s % TOKEN_TILE == 0 and TOKEN_TILE >= PREV_KEYS and TOKEN_TILE % Q_SUB == 0
    row = lambda a: a.reshape(1, -1).astype(F32)
    grp = np.arange(SG_WIDTH) // SG_GROUP_DIM
    avg = jnp.asarray((grp[:, None] == grp[None, :]) / SG_GROUP_DIM, BF16)

    xf = x.reshape(b * s, d)
    for l in range(depth):
        wl = w_in[l].astype(BF16)
        a = ATT_WIDTH
        wq, wk, wv = wl[:, :a], wl[:, a:2 * a], wl[:, 2 * a:3 * a]
        wu, wvs = wl[:, 3 * a:3 * a + SG_WIDTH], wl[:, 3 * a + SG_WIDTH:3 * a + 2 * SG_WIDTH]
        g0 = 3 * a + 2 * SG_WIDTH
        wga, wgb = wl[:, g0:g0 + d], wl[:, g0 + d:g0 + 2 * d]
        sgw = sg_w[l].reshape(SG_GROUPS // 2, 2 * SG_BLOCK, SG_BLOCK)
        sgb = jnp.repeat(sg_b[l].T, SG_GROUP_DIM, axis=1)
        wxkv = w_xkv[l].astype(BF16)

        bias = _bias_table(rel_bias[l])
        kmT, vm = _mem_kv(mem, row(norm_mem_g[l]), wxkv[:, :d].T, wxkv[:, d:])
        q, kT, v, ga, mb = _inproj(
            xf, row(norm_mix_g[l]), wq, wk.T, wv, wu, wvs, wga, wgb, avg,
            row(sg_ln_g[l]), row(sg_ln_b[l]), sgw, sgb, w_branch_sg[l].astype(BF16))
        xf = _attn(xf, q, kT, v, ga, mb, bias, w_branch_att[l].astype(BF16),
                   w_out[l].astype(BF16), row(norm_xattn_g[l]), w_xq[l].astype(BF16),
                   kmT, vm, w_xo[l].astype(BF16), s)
        assert depth == 1
        xf = _ffn(xf, row(norm_ffn_g[l]), w_ffn_in[l].astype(BF16), w_ffn_out[l].astype(BF16),
                  row(norm_final_g))
    return xf.reshape(b, s, d)
```

```python
import functools

import jax
import jax.numpy as jnp
import numpy as np
from jax import lax
from jax.experimental import pallas as pl
from jax.experimental.pallas import tpu as pltpu

CHUNK = 64
N_PREV_CHUNKS = 8
ATT_HEADS = 8
ATT_HEAD_DIM = 64
ATT_WIDTH = ATT_HEADS * ATT_HEAD_DIM
REL_CLIP = 128
SG_BLOCK = 128
SG_GROUPS = 8
SG_GROUP_DIM = 64
SG_WIDTH = SG_GROUPS * SG_GROUP_DIM
XATT_HEADS = 4
EPS = 1e-6
NEG_INF = -1e30

LANES = 128
VMEM_LIMIT_BYTES = 56 * 1024 * 1024

TOKEN_TILE = 512
Q_SUB = 2 * CHUNK
PREV_KEYS = N_PREV_CHUNKS * CHUNK
WINDOW = PREV_KEYS + Q_SUB
FF_CHUNK = 256
SCORE_LOOKAHEAD = 2

F32 = jnp.float32
BF16 = jnp.bfloat16
_NT = (((1,), (1,)), ((), ()))


def _dot(a, b):
    return jnp.dot(a, b, preferred_element_type=F32)


def _dot_nt(a, b):
    return lax.dot_general(a, b, _NT, preferred_element_type=F32)


def _rmsnorm(x, g):
    return x * lax.rsqrt(jnp.mean(x * x, axis=-1, keepdims=True) + EPS) * g


def _const_spec(shape):
    zeros = (0,) * len(shape)
    return pl.BlockSpec(shape, lambda *_: zeros, pipeline_mode=pl.Buffered(1))


def _params():
    return pltpu.CompilerParams(dimension_semantics=("arbitrary",),
                                vmem_limit_bytes=VMEM_LIMIT_BYTES)


def _bias_kernel(rb_ref, o_ref):
    var_cols = WINDOW - (PREV_KEYS - REL_CLIP)
    col0 = WINDOW - var_cols
    qi = lax.broadcasted_iota(jnp.int32, (Q_SUB, var_cols), 0)
    km = lax.broadcasted_iota(jnp.int32, (Q_SUB, var_cols), 1) + col0
    idx = jnp.clip(PREV_KEYS + qi - km, -REL_CLIP, REL_CLIP) + REL_CLIP
    qc = lax.broadcasted_iota(jnp.int32, (Q_SUB, WINDOW), 0) // CHUNK
    kc = lax.broadcasted_iota(jnp.int32, (Q_SUB, WINDOW), 1) // CHUNK
    valid = (kc >= qc) & (kc <= qc + N_PREV_CHUNKS)
    for h in range(ATT_HEADS):
        def body(j, t, h=h):
            return jnp.where(idx == j, rb_ref[h, j], t)
        t = lax.fori_loop(0, 2 * REL_CLIP + 1, body, jnp.zeros((Q_SUB, var_cols), F32))
        far = jnp.full((Q_SUB, col0), rb_ref[h, 2 * REL_CLIP], F32)
        o_ref[h] = jnp.where(valid, jnp.concatenate([far, t], axis=1), NEG_INF)


def _bias_table(rel_bias):
    return pl.pallas_call(
        _bias_kernel,
        out_shape=jax.ShapeDtypeStruct((ATT_HEADS, Q_SUB, WINDOW), F32),
        in_specs=[pl.BlockSpec(memory_space=pltpu.SMEM)],
        out_specs=pl.BlockSpec(memory_space=pltpu.VMEM),
        name="bias_table",
    )(rel_bias)


def _memkv_kernel(mem_ref, g_ref, wkT_ref, wv_ref, kT_ref, v_ref):
    mn = _rmsnorm(mem_ref[0], g_ref[...]).astype(BF16)
    kT_ref[0] = _dot_nt(wkT_ref[...], mn).astype(BF16)
    v_ref[0] = _dot(mn, wv_ref[...]).astype(BF16)


def _mem_kv(mem, g, wkT, wv):
    b, m, d = mem.shape
    return pl.pallas_call(
        _memkv_kernel,
        grid=(b,),
        out_shape=(jax.ShapeDtypeStruct((b, d, m), BF16), jax.ShapeDtypeStruct((b, m, d), BF16)),
        in_specs=[pl.BlockSpec((1, m, d), lambda i: (i, 0, 0)),
                  _const_spec((1, d)), _const_spec((d, d)), _const_spec((d, d))],
        out_specs=(pl.BlockSpec((1, d, m), lambda i: (i, 0, 0)),
                   pl.BlockSpec((1, m, d), lambda i: (i, 0, 0))),
        compiler_params=_params(),
        name="mem_kv",
    )(mem, g, wkT, wv)


def _inproj_kernel(x_ref, g_ref, wq_ref, wkT_ref, wv_ref, wu_ref, wvs_ref, wga_ref, wgb_ref,
                   avg_ref, lng_ref, lnb_ref, sgw_ref, sgb_ref, wbsg_ref,
                   q_ref, kT_ref, v_ref, ga_ref, mb_ref, wm_scr, ysg_scr):
    pair_rows = 2 * SG_BLOCK

    @pl.when(pl.program_id(0) == 0)
    def _():
        t = lax.broadcasted_iota(jnp.int32, (pair_rows, SG_BLOCK), 0) % SG_BLOCK
        s = lax.broadcasted_iota(jnp.int32, (pair_rows, SG_BLOCK), 1)
        mask = (s // CHUNK) <= (t // CHUNK)
        for j in range(SG_GROUPS // 2):
            wm_scr[j] = jnp.where(mask, sgw_ref[j], 0.0).astype(BF16)

    hb = _rmsnorm(x_ref[...], g_ref[...]).astype(BF16)
    vs_raw = _dot(hb, wvs_ref[...])
    u_raw = _dot(hb, wu_ref[...])
    q_ref[...] = (_dot(hb, wq_ref[...]) * (ATT_HEAD_DIM ** -0.5)).astype(BF16)
    vs = jax.nn.gelu(vs_raw)
    mean = _dot(vs.astype(BF16), avg_ref[...])
    kT_ref[...] = _dot_nt(wkT_ref[...], hb).astype(BF16)
    dev = vs - mean
    var = _dot((dev * dev).astype(BF16), avg_ref[...])
    v_ref[...] = _dot(hb, wv_ref[...]).astype(BF16)
    ga_raw = _dot(hb, wga_ref[...])
    vln = (dev * lax.rsqrt(var + EPS) * lng_ref[...] + lnb_ref[...]).astype(BF16)
    u = jax.nn.gelu(u_raw)

    lane = lax.broadcasted_iota(jnp.int32, (SG_BLOCK, LANES), 1)
    for n in range(x_ref.shape[0] // SG_BLOCK):
        rows = slice(n * SG_BLOCK, (n + 1) * SG_BLOCK)
        for j in range(SG_GROUPS // 2):
            cols = slice(j * LANES, (j + 1) * LANES)
            r = _dot(wm_scr[j], vln[rows, cols])
            sv = jnp.where(lane < SG_GROUP_DIM, r[:SG_BLOCK], r[SG_BLOCK:]) + sgb_ref[:, cols]
            ysg_scr[rows, cols] = (u[rows, cols] * sv).astype(BF16)

    gate_b = jax.nn.sigmoid(_dot(hb, wgb_ref[...]))
    mb_ref[...] = (gate_b * _dot(ysg_scr[...], wbsg_ref[...])).astype(BF16)
    ga_ref[...] = jax.nn.sigmoid(ga_raw).astype(BF16)


def _inproj(x, g, wq, wkT, wv, wu, wvs, wga, wgb, avg, lng, lnb, sgw, sgb, wbsg):
    n, d = x.shape
    tm = TOKEN_TILE
    tile = lambda w: pl.BlockSpec((tm, w), lambda i: (i, 0))
    return pl.pallas_call(
        _inproj_kernel,
        grid=(n // tm,),
        out_shape=(jax.ShapeDtypeStruct((n, ATT_WIDTH), BF16),
                   jax.ShapeDtypeStruct((ATT_WIDTH, n), BF16),
                   jax.ShapeDtypeStruct((n, ATT_WIDTH), BF16),
                   jax.ShapeDtypeStruct((n, d), BF16),
                   jax.ShapeDtypeStruct((n, d), BF16)),
        in_specs=[tile(d), _const_spec((1, d)),
                  _const_spec(wq.shape), _const_spec(wkT.shape), _const_spec(wv.shape),
                  _const_spec(wu.shape), _const_spec(wvs.shape), _const_spec(wga.shape),
                  _const_spec(wgb.shape), _const_spec(avg.shape), _const_spec(lng.shape),
                  _const_spec(lnb.shape), _const_spec(sgw.shape), _const_spec(sgb.shape),
                  _const_spec(wbsg.shape)],
        out_specs=(tile(ATT_WIDTH), pl.BlockSpec((ATT_WIDTH, tm), lambda i: (0, i)),
                   tile(ATT_WIDTH), tile(d), tile(d)),
        scratch_shapes=[pltpu.VMEM((SG_GROUPS // 2, 2 * SG_BLOCK, SG_BLOCK), BF16),
                        pltpu.VMEM((tm, SG_WIDTH), BF16)],
        compiler_params=_params(),
        name="inproj_sgu",
    )(x, g, wq, wkT, wv, wu, wvs, wga, wgb, avg, lng, lnb, sgw, sgb, wbsg)


def _attn_kernel(tiles_per_seq, x_ref, q_ref, kTp_ref, kTc_ref, vp_ref, vc_ref, ga_ref, mb_ref,
                 bias_ref, wbatt_ref, wout_ref, gx_ref, wxq_ref, kmT_ref, vm_ref, wxo_ref,
                 o_ref, kT_scr, v_scr, yatt_scr, xo_scr):
    tm = x_ref.shape[0]
    d = x_ref.shape[1]
    first_tile = (pl.program_id(0) % tiles_per_seq) == 0

    kT_scr[:, :tm] = kTp_ref[...]
    kT_scr[:, tm:] = kTc_ref[...]
    v_scr[:tm] = vp_ref[...]
    v_scr[tm:] = vc_ref[...]

    lane = lax.broadcasted_iota(jnp.int32, (Q_SUB, LANES), 1)
    col = lax.broadcasted_iota(jnp.int32, (1, WINDOW), 1)
    pens = [jnp.where(col < jnp.where(first_tile, max(PREV_KEYS - j * Q_SUB, 0), 0), NEG_INF, 0.0)
            for j in range(tm // Q_SUB)]

    def window(j, h):
        q0 = j * Q_SUB
        cols = slice((h // 2) * LANES, (h // 2 + 1) * LANES)
        return q0, tm + q0 - PREV_KEYS, cols

    def scores(j, h):
        q0, w0, cols = window(j, h)
        qp = q_ref[q0:q0 + Q_SUB, cols]
        qz = jnp.where((lane < ATT_HEAD_DIM) == (h % 2 == 0), qp, jnp.zeros_like(qp))
        return _dot(qz, kT_scr[cols, w0:w0 + WINDOW]) + bias_ref[h] + pens[j]

    def attend(j, h, s):
        _, w0, cols = window(j, h)
        e = jnp.exp(s - jnp.max(s, axis=-1, keepdims=True))
        den = jnp.sum(e, axis=-1, keepdims=True)
        return _dot(e.astype(BF16), v_scr[w0:w0 + WINDOW, cols]) / den

    steps = [(j, h) for j in range(tm // Q_SUB) for h in range(ATT_HEADS)]
    pending = [scores(*st) for st in steps[:SCORE_LOOKAHEAD]]
    for k, (j, h) in enumerate(steps):
        if k + SCORE_LOOKAHEAD < len(steps):
            pending.append(scores(*steps[k + SCORE_LOOKAHEAD]))
        o = attend(j, h, pending.pop(0))
        if h % 2 == 0:
            o_even = o
        else:
            q0, _, cols = window(j, h)
            y = jnp.where(lane < ATT_HEAD_DIM, o_even, o)
            yatt_scr[q0:q0 + Q_SUB, cols] = y.astype(BF16)

    att = _dot(yatt_scr[...], wbatt_ref[...])
    merged = ga_ref[...].astype(F32) * att + mb_ref[...].astype(F32)
    x1 = x_ref[...] + _dot(merged.astype(BF16), wout_ref[...])

    hx = _rmsnorm(x1, gx_ref[...]).astype(BF16)
    xd = d // XATT_HEADS
    qx = (_dot(hx, wxq_ref[...]) * (xd ** -0.5)).astype(BF16)
    for h in range(XATT_HEADS):
        cols = slice(h * xd, (h + 1) * xd)
        s = _dot(qx[:, cols], kmT_ref[0, cols, :])
        e = jnp.exp(s - jnp.max(s, axis=-1, keepdims=True))
        den = jnp.sum(e, axis=-1, keepdims=True)
        xo_scr[:, cols] = (_dot(e.astype(BF16), vm_ref[0, :, cols]) / den).astype(BF16)
    o_ref[...] = x1 + _dot(xo_scr[...], wxo_ref[...])


def _attn(x, q, kT, v, ga, mb, bias, wbatt, wout, gx, wxq, kmT, vm, wxo, seq):
    n, d = x.shape
    tm = TOKEN_TILE
    tps = seq // tm
    mem_len = vm.shape[1]
    tile = lambda w: pl.BlockSpec((tm, w), lambda i: (i, 0))
    prev = lambda i: jnp.maximum(i - 1, 0)
    return pl.pallas_call(
        functools.partial(_attn_kernel, tps),
        grid=(n // tm,),
        out_shape=jax.ShapeDtypeStruct((n, d), F32),
        in_specs=[tile(d), tile(ATT_WIDTH),
                  pl.BlockSpec((ATT_WIDTH, tm), lambda i: (0, prev(i))),
                  pl.BlockSpec((ATT_WIDTH, tm), lambda i: (0, i)),
                  pl.BlockSpec((tm, ATT_WIDTH), lambda i: (prev(i), 0)),
                  tile(ATT_WIDTH), tile(d), tile(d),
                  _const_spec(bias.shape), _const_spec(wbatt.shape), _const_spec(wout.shape),
                  _const_spec((1, d)), _const_spec(wxq.shape),
                  pl.BlockSpec((1, d, mem_len), lambda i: (i // tps, 0, 0)),
                  pl.BlockSpec((1, mem_len, d), lambda i: (i // tps, 0, 0)),
                  _const_spec(wxo.shape)],
        out_specs=tile(d),
        scratch_shapes=[pltpu.VMEM((ATT_WIDTH, 2 * tm), BF16),
                        pltpu.VMEM((2 * tm, ATT_WIDTH), BF16),
                        pltpu.VMEM((tm, ATT_WIDTH), BF16),
                        pltpu.VMEM((tm, d), BF16)],
        compiler_params=_params(),
        name="attn_merge_xattn",
    )(x, q, kT, kT, v, v, ga, mb, bias, wbatt, wout, gx, wxq, kmT, vm, wxo)


def _ffn_kernel(x_ref, g_ref, wfi_ref, wfo_ref, gfin_ref, o_ref, a_scr):
    d_ff = wfo_ref.shape[0]
    x = x_ref[...]
    hb = _rmsnorm(x, g_ref[...]).astype(BF16)
    for c in range(d_ff // FF_CHUNK):
        gate = _dot(hb, wfi_ref[:, c * FF_CHUNK:(c + 1) * FF_CHUNK])
        up = _dot(hb, wfi_ref[:, d_ff + c * FF_CHUNK:d_ff + (c + 1) * FF_CHUNK])
        a_scr[:, c * FF_CHUNK:(c + 1) * FF_CHUNK] = (jax.nn.silu(gate) * up).astype(BF16)
    x3 = x + _dot(a_scr[...], wfo_ref[...])
    o_ref[...] = _rmsnorm(x3, gfin_ref[...])


def _ffn(x, g, wfi, wfo, gfin):
    n, d = x.shape
    tm = TOKEN_TILE
    d_ff = wfo.shape[0]
    assert d_ff % FF_CHUNK == 0
    tile = pl.BlockSpec((tm, d), lambda i: (i, 0))
    return pl.pallas_call(
        _ffn_kernel,
        grid=(n // tm,),
        out_shape=jax.ShapeDtypeStruct((n, d), F32),
        in_specs=[tile, _const_spec((1, d)), _const_spec(wfi.shape), _const_spec(wfo.shape),
                  _const_spec((1, d))],
        out_specs=tile,
        scratch_shapes=[pltpu.VMEM((tm, d_ff), BF16)],
        compiler_params=_params(),
        name="ffn_final_norm",
    )(x, g, wfi, wfo, gfin)


def kernel(x, mem, norm_mix_g, w_in, rel_bias, sg_ln_g, sg_ln_b, sg_w, sg_b, w_branch_att,
           w_branch_sg, w_out, norm_xattn_g, norm_mem_g, w_xq, w_xkv, w_xo, norm_ffn_g,
           w_ffn_in, w_ffn_out, norm_final_g):
    b, s, d = x.shape
    depth = w_in.shape[0]
    assert s % TOKEN_TILE == 0 and TOKEN_TILE >= PREV_KEYS and TOKEN_TILE % Q_SUB == 0
    row = lambda a: a.reshape(1, -1).astype(F32)
    grp = np.arange(SG_WIDTH) // SG_GROUP_DIM
    avg = jnp.asarray((grp[:, None] == grp[None, :]) / SG_GROUP_DIM, BF16)

    xf = x.reshape(b * s, d)
    for l in range(depth):
        wl = w_in[l].astype(BF16)
        a = ATT_WIDTH
        wq, wk, wv = wl[:, :a], wl[:, a:2 * a], wl[:, 2 * a:3 * a]
        wu, wvs = wl[:, 3 * a:3 * a + SG_WIDTH], wl[:, 3 * a + SG_WIDTH:3 * a + 2 * SG_WIDTH]
        g0 = 3 * a + 2 * SG_WIDTH
        wga, wgb = wl[:, g0:g0 + d], wl[:, g0 + d:g0 + 2 * d]
        sgw = sg_w[l].reshape(SG_GROUPS // 2, 2 * SG_BLOCK, SG_BLOCK)
        sgb = jnp.repeat(sg_b[l].T, SG_GROUP_DIM, axis=1)
        wxkv = w_xkv[l].astype(BF16)

        bias = _bias_table(rel_bias[l])
        kmT, vm = _mem_kv(mem, row(norm_mem_g[l]), wxkv[:, :d].T, wxkv[:, d:])
        q, kT, v, ga, mb = _inproj(
            xf, row(norm_mix_g[l]), wq, wk.T, wv, wu, wvs, wga, wgb, avg,
            row(sg_ln_g[l]), row(sg_ln_b[l]), sgw, sgb, w_branch_sg[l].astype(BF16))
        xf = _attn(xf, q, kT, v, ga, mb, bias, w_branch_att[l].astype(BF16),
                   w_out[l].astype(BF16), row(norm_xattn_g[l]), w_xq[l].astype(BF16),
                   kmT, vm, w_xo[l].astype(BF16), s)
        assert depth == 1
        xf = _ffn(xf, row(norm_ffn_g[l]), w_ffn_in[l].astype(BF16), w_ffn_out[l].astype(BF16),
                  row(norm_final_g))
    return xf.reshape(b, s, d)
```

```python
import functools

import jax
import jax.numpy as jnp
import numpy as np
from jax import lax
from jax.experimental import pallas as pl
from jax.experimental.pallas import tpu as pltpu

CHUNK = 64
N_PREV_CHUNKS = 8
ATT_HEADS = 8
ATT_HEAD_DIM = 64
ATT_WIDTH = ATT_HEADS * ATT_HEAD_DIM
REL_CLIP = 128
SG_BLOCK = 128
SG_GROUPS = 8
SG_GROUP_DIM = 64
SG_WIDTH = SG_GROUPS * SG_GROUP_DIM
XATT_HEADS = 4
EPS = 1e-6
NEG_INF = -1e30
LOG2E = 1.4426950408889634

LANES = 128
VMEM_LIMIT_BYTES = 56 * 1024 * 1024

TOKEN_TILE = 512
Q_SUB = 2 * CHUNK
PREV_KEYS = N_PREV_CHUNKS * CHUNK
WINDOW = PREV_KEYS + Q_SUB
FF_CHUNK = 256
SCORE_LOOKAHEAD = 2

F32 = jnp.float32
BF16 = jnp.bfloat16
_NT = (((1,), (1,)), ((), ()))


def _dot(a, b):
    return jnp.dot(a, b, preferred_element_type=F32)


def _dot_nt(a, b):
    return lax.dot_general(a, b, _NT, preferred_element_type=F32)


def _rmsnorm(x, g):
    return x * lax.rsqrt(jnp.mean(x * x, axis=-1, keepdims=True) + EPS) * g


def _const_spec(shape):
    zeros = (0,) * len(shape)
    return pl.BlockSpec(shape, lambda *_: zeros, pipeline_mode=pl.Buffered(1))


def _params():
    return pltpu.CompilerParams(dimension_semantics=("arbitrary",),
                                vmem_limit_bytes=VMEM_LIMIT_BYTES)


def _bias_kernel(rb_ref, o_ref):
    width = WINDOW + Q_SUB
    c = lax.broadcasted_iota(jnp.int32, (1, width), 1)
    idx = jnp.clip(WINDOW - c, -REL_CLIP, REL_CLIP) + REL_CLIP
    diag = jnp.zeros((ATT_HEADS, width), F32)
    for j in range(2 * REL_CLIP + 1):
        diag = jnp.where(idx == j, rb_ref[:, j:j + 1], diag)
    diag = diag * LOG2E

    row = lax.broadcasted_iota(jnp.int32, (Q_SUB, width), 0)
    qc = lax.broadcasted_iota(jnp.int32, (Q_SUB, WINDOW), 0) // CHUNK
    kc = lax.broadcasted_iota(jnp.int32, (Q_SUB, WINDOW), 1) // CHUNK
    valid = (kc >= qc) & (kc <= qc + N_PREV_CHUNKS)
    for h in range(ATT_HEADS):
        t = jnp.broadcast_to(diag[h:h + 1, :], (Q_SUB, width))
        for bit in range(Q_SUB.bit_length() - 1):
            t = jnp.where(((row >> bit) & 1) == 1, pltpu.roll(t, 1 << bit, axis=1), t)
        o_ref[h] = jnp.where(valid, t[:, Q_SUB:], NEG_INF)
    o_ref[ATT_HEADS] = jnp.full((Q_SUB, WINDOW), NEG_INF, F32)


def _bias_table(rel_bias):
    return pl.pallas_call(
        _bias_kernel,
        out_shape=jax.ShapeDtypeStruct((ATT_HEADS + 1, Q_SUB, WINDOW), F32),
        in_specs=[pl.BlockSpec(memory_space=pltpu.VMEM)],
        out_specs=pl.BlockSpec(memory_space=pltpu.VMEM),
        name="bias_table",
    )(rel_bias)


def _memkv_kernel(mem_ref, g_ref, wkT_ref, wv_ref, kT_ref, v_ref):
    mn = _rmsnorm(mem_ref[0], g_ref[...]).astype(BF16)
    kT_ref[0] = _dot_nt(wkT_ref[...], mn).astype(BF16)
    v_ref[0] = _dot(mn, wv_ref[...]).astype(BF16)


def _mem_kv(mem, g, wkT, wv):
    b, m, d = mem.shape
    return pl.pallas_call(
        _memkv_kernel,
        grid=(b,),
        out_shape=(jax.ShapeDtypeStruct((b, d, m), BF16), jax.ShapeDtypeStruct((b, m, d), BF16)),
        in_specs=[pl.BlockSpec((1, m, d), lambda i: (i, 0, 0)),
                  _const_spec((1, d)), _const_spec((d, d)), _const_spec((d, d))],
        out_specs=(pl.BlockSpec((1, d, m), lambda i: (i, 0, 0)),
                   pl.BlockSpec((1, m, d), lambda i: (i, 0, 0))),
        compiler_params=_params(),
        name="mem_kv",
    )(mem, g, wkT, wv)


def _inproj_kernel(x_ref, g_ref, wq_ref, wkT_ref, wv_ref, wu_ref, wvs_ref, wga_ref, wgb_ref,
                   avg_ref, lng_ref, lnb_ref, sgw_ref, sgb_ref, wbsg_ref,
                   q_ref, kT_ref, v_ref, ga_ref, mb_ref, wm_scr, ysg_scr):
    pair_rows = 2 * SG_BLOCK

    @pl.when(pl.program_id(0) == 0)
    def _():
        t = lax.broadcasted_iota(jnp.int32, (pair_rows, SG_BLOCK), 0) % SG_BLOCK
        s = lax.broadcasted_iota(jnp.int32, (pair_rows, SG_BLOCK), 1)
        mask = (s // CHUNK) <= (t // CHUNK)
        for j in range(SG_GROUPS // 2):
            wm_scr[j] = jnp.where(mask, sgw_ref[j], 0.0).astype(BF16)

    hb = _rmsnorm(x_ref[...], g_ref[...]).astype(BF16)
    vs_raw = _dot(hb, wvs_ref[...])
    u_raw = _dot(hb, wu_ref[...])
    q_ref[...] = (_dot(hb, wq_ref[...]) * (ATT_HEAD_DIM ** -0.5 * LOG2E)).astype(BF16)
    vs = jax.nn.gelu(vs_raw)
    mean = _dot(vs.astype(BF16), avg_ref[...])
    kT_ref[...] = _dot_nt(wkT_ref[...], hb).astype(BF16)
    dev = vs - mean
    var = _dot((dev * dev).astype(BF16), avg_ref[...])
    v_ref[...] = _dot(hb, wv_ref[...]).astype(BF16)
    ga_raw = _dot(hb, wga_ref[...])
    vln = (dev * lax.rsqrt(var + EPS) * lng_ref[...] + lnb_ref[...]).astype(BF16)
    u = jax.nn.gelu(u_raw)

    lane = lax.broadcasted_iota(jnp.int32, (SG_BLOCK, LANES), 1)
    for n in range(x_ref.shape[0] // SG_BLOCK):
        rows = slice(n * SG_BLOCK, (n + 1) * SG_BLOCK)
        for j in range(SG_GROUPS // 2):
            cols = slice(j * LANES, (j + 1) * LANES)
            r = _dot(wm_scr[j], vln[rows, cols])
            sv = jnp.where(lane < SG_GROUP_DIM, r[:SG_BLOCK], r[SG_BLOCK:]) + sgb_ref[:, cols]
            ysg_scr[rows, cols] = (u[rows, cols] * sv).astype(BF16)

    gate_b = jax.nn.sigmoid(_dot(hb, wgb_ref[...]))
    mb_ref[...] = (gate_b * _dot(ysg_scr[...], wbsg_ref[...])).astype(BF16)
    ga_ref[...] = jax.nn.sigmoid(ga_raw).astype(BF16)


def _inproj(x, g, wq, wkT, wv, wu, wvs, wga, wgb, avg, lng, lnb, sgw, sgb, wbsg):
    n, d = x.shape
    tm = TOKEN_TILE
    tile = lambda w: pl.BlockSpec((tm, w), lambda i: (i, 0))
    return pl.pallas_call(
        _inproj_kernel,
        grid=(n // tm,),
        out_shape=(jax.ShapeDtypeStruct((n, ATT_WIDTH), BF16),
                   jax.ShapeDtypeStruct((ATT_WIDTH, n), BF16),
                   jax.ShapeDtypeStruct((n, ATT_WIDTH), BF16),
                   jax.ShapeDtypeStruct((n, d), BF16),
                   jax.ShapeDtypeStruct((n, d), BF16)),
        in_specs=[tile(d), _const_spec((1, d)),
                  _const_spec(wq.shape), _const_spec(wkT.shape), _const_spec(wv.shape),
                  _const_spec(wu.shape), _const_spec(wvs.shape), _const_spec(wga.shape),
                  _const_spec(wgb.shape), _const_spec(avg.shape), _const_spec(lng.shape),
                  _const_spec(lnb.shape), _const_spec(sgw.shape), _const_spec(sgb.shape),
                  _const_spec(wbsg.shape)],
        out_specs=(tile(ATT_WIDTH), pl.BlockSpec((ATT_WIDTH, tm), lambda i: (0, i)),
                   tile(ATT_WIDTH), tile(d), tile(d)),
        scratch_shapes=[pltpu.VMEM((SG_GROUPS // 2, 2 * SG_BLOCK, SG_BLOCK), BF16),
                        pltpu.VMEM((tm, SG_WIDTH), BF16)],
        compiler_params=_params(),
        name="inproj_sgu",
    )(x, g, wq, wkT, wv, wu, wvs, wga, wgb, avg, lng, lnb, sgw, sgb, wbsg)


def _attn_kernel(tiles_per_seq, x_ref, q_ref, kTp_ref, kTc_ref, vp_ref, vc_ref, ga_ref, mb_ref,
                 bias_ref, wbatt_ref, wout_ref, gx_ref, wxq_ref, kmT_ref, vm_ref, wxo_ref,
                 o_ref, kT_scr, v_scr, yatt_scr, xo_scr):
    tm = x_ref.shape[0]
    d = x_ref.shape[1]
    first_tile = (pl.program_id(0) % tiles_per_seq) == 0

    kT_scr[:, :tm] = kTp_ref[...]
    kT_scr[:, tm:] = kTc_ref[...]
    vcat = jnp.concatenate([vp_ref[...], vc_ref[...]], axis=0)
    low_half = (lax.broadcasted_iota(jnp.int32, vcat.shape, 1) % LANES) < ATT_HEAD_DIM
    v_scr[0] = jnp.where(low_half, vcat, jnp.ones_like(vcat))
    v_scr[1] = jnp.where(low_half, jnp.ones_like(vcat), vcat)

    lane = lax.broadcasted_iota(jnp.int32, (Q_SUB, LANES), 1)
    n_blocks = WINDOW // LANES

    def window(j, h):
        q0 = j * Q_SUB
        cols = slice((h // 2) * LANES, (h // 2 + 1) * LANES)
        return q0, tm + q0 - PREV_KEYS, cols

    def scores(j, h):
        q0, w0, cols = window(j, h)
        qp = q_ref[q0:q0 + Q_SUB, cols]
        qz = jnp.where((lane < ATT_HEAD_DIM) == (h % 2 == 0), qp, jnp.zeros_like(qp))
        s = _dot(qz, kT_scr[cols, w0:w0 + WINDOW])
        blocks = []
        for b in range(n_blocks):
            before_start = (b + 1) * LANES <= PREV_KEYS - q0
            entry = jnp.where(first_tile, ATT_HEADS, h) if before_start else h
            blk = slice(b * LANES, (b + 1) * LANES)
            blocks.append(s[:, blk] + bias_ref[entry, :, blk])
        return jnp.concatenate(blocks, axis=1)

    def attend(j, h, s):
        _, w0, cols = window(j, h)
        p = jnp.exp2(s - jnp.max(s, axis=-1, keepdims=True)).astype(BF16)
        o = _dot(p, v_scr[h % 2, w0:w0 + WINDOW, cols])
        return o / pltpu.roll(o, ATT_HEAD_DIM, axis=1)

    steps = [(j, h) for j in range(tm // Q_SUB) for h in range(ATT_HEADS)]
    pending = [scores(*st) for st in steps[:SCORE_LOOKAHEAD]]
    for k, (j, h) in enumerate(steps):
        if k + SCORE_LOOKAHEAD < len(steps):
            pending.append(scores(*steps[k + SCORE_LOOKAHEAD]))
        o = attend(j, h, pending.pop(0))
        if h % 2 == 0:
            o_even = o
        else:
            q0, _, cols = window(j, h)
            y = jnp.where(lane < ATT_HEAD_DIM, o_even, o)
            yatt_scr[q0:q0 + Q_SUB, cols] = y.astype(BF16)

    att = _dot(yatt_scr[...], wbatt_ref[...])
    merged = ga_ref[...].astype(F32) * att + mb_ref[...].astype(F32)
    x1 = x_ref[...] + _dot(merged.astype(BF16), wout_ref[...])

    hx = _rmsnorm(x1, gx_ref[...]).astype(BF16)
    xd = d // XATT_HEADS
    qx = (_dot(hx, wxq_ref[...]) * (xd ** -0.5)).astype(BF16)
    for h in range(XATT_HEADS):
        cols = slice(h * xd, (h + 1) * xd)
        s = _dot(qx[:, cols], kmT_ref[0, cols, :])
        e = jnp.exp(s - jnp.max(s, axis=-1, keepdims=True))
        den = jnp.sum(e, axis=-1, keepdims=True)
        xo_scr[:, cols] = (_dot(e.astype(BF16), vm_ref[0, :, cols]) / den).astype(BF16)
    o_ref[...] = x1 + _dot(xo_scr[...], wxo_ref[...])


def _attn(x, q, kT, v, ga, mb, bias, wbatt, wout, gx, wxq, kmT, vm, wxo, seq):
    n, d = x.shape
    tm = TOKEN_TILE
    tps = seq // tm
    mem_len = vm.shape[1]
    tile = lambda w: pl.BlockSpec((tm, w), lambda i: (i, 0))
    prev = lambda i: jnp.maximum(i - 1, 0)
    return pl.pallas_call(
        functools.partial(_attn_kernel, tps),
        grid=(n // tm,),
        out_shape=jax.ShapeDtypeStruct((n, d), F32),
        in_specs=[tile(d), tile(ATT_WIDTH),
                  pl.BlockSpec((ATT_WIDTH, tm), lambda i: (0, prev(i))),
                  pl.BlockSpec((ATT_WIDTH, tm), lambda i: (0, i)),
                  pl.BlockSpec((tm, ATT_WIDTH), lambda i: (prev(i), 0)),
                  tile(ATT_WIDTH), tile(d), tile(d),
                  _const_spec(bias.shape), _const_spec(wbatt.shape), _const_spec(wout.shape),
                  _const_spec((1, d)), _const_spec(wxq.shape),
                  pl.BlockSpec((1, d, mem_len), lambda i: (i // tps, 0, 0)),
                  pl.BlockSpec((1, mem_len, d), lambda i: (i // tps, 0, 0)),
                  _const_spec(wxo.shape)],
        out_specs=tile(d),
        scratch_shapes=[pltpu.VMEM((ATT_WIDTH, 2 * tm), BF16),
                        pltpu.VMEM((2, 2 * tm, ATT_WIDTH), BF16),
                        pltpu.VMEM((tm, ATT_WIDTH), BF16),
                        pltpu.VMEM((tm, d), BF16)],
        compiler_params=_params(),
        name="attn_merge_xattn",
    )(x, q, kT, kT, v, v, ga, mb, bias, wbatt, wout, gx, wxq, kmT, vm, wxo)


def _ffn_kernel(x_ref, g_ref, wfi_ref, wfo_ref, gfin_ref, o_ref, a_scr):
    d_ff = wfo_ref.shape[0]
    x = x_ref[...]
    hb = _rmsnorm(x, g_ref[...]).astype(BF16)
    for c in range(d_ff // FF_CHUNK):
        gate = _dot(hb, wfi_ref[:, c * FF_CHUNK:(c + 1) * FF_CHUNK])
        up = _dot(hb, wfi_ref[:, d_ff + c * FF_CHUNK:d_ff + (c + 1) * FF_CHUNK])
        a_scr[:, c * FF_CHUNK:(c + 1) * FF_CHUNK] = (jax.nn.silu(gate) * up).astype(BF16)
    x3 = x + _dot(a_scr[...], wfo_ref[...])
    o_ref[...] = _rmsnorm(x3, gfin_ref[...])


def _ffn(x, g, wfi, wfo, gfin):
    n, d = x.shape
    tm = TOKEN_TILE
    d_ff = wfo.shape[0]
    assert d_ff % FF_CHUNK == 0
    tile = pl.BlockSpec((tm, d), lambda i: (i, 0))
    return pl.pallas_call(
        _ffn_kernel,
        grid=(n // tm,),
        out_shape=jax.ShapeDtypeStruct((n, d), F32),
        in_specs=[tile, _const_spec((1, d)), _const_spec(wfi.shape), _const_spec(wfo.shape),
                  _const_spec((1, d))],
        out_specs=tile,
        scratch_shapes=[pltpu.VMEM((tm, d_ff), BF16)],
        compiler_params=_params(),
        name="ffn_final_norm",
    )(x, g, wfi, wfo, gfin)


def kernel(x, mem, norm_mix_g, w_in, rel_bias, sg_ln_g, sg_ln_b, sg_w, sg_b, w_branch_att,
           w_branch_sg, w_out, norm_xattn_g, norm_mem_g, w_xq, w_xkv, w_xo, norm_ffn_g,
           w_ffn_in, w_ffn_out, norm_final_g):
    b, s, d = x.shape
    depth = w_in.shape[0]
    assert s % TOKEN_TILE == 0 and TOKEN_TILE >= PREV_KEYS and TOKEN_TILE % Q_SUB == 0
    row = lambda a: a.reshape(1, -1).astype(F32)
    grp = np.arange(SG_WIDTH) // SG_GROUP_DIM
    avg = jnp.asarray((grp[:, None] == grp[None, :]) / SG_GROUP_DIM, BF16)

    xf = x.reshape(b * s, d)
    for l in range(depth):
        wl = w_in[l].astype(BF16)
        a = ATT_WIDTH
        wq, wk, wv = wl[:, :a], wl[:, a:2 * a], wl[:, 2 * a:3 * a]
        wu, wvs = wl[:, 3 * a:3 * a + SG_WIDTH], wl[:, 3 * a + SG_WIDTH:3 * a + 2 * SG_WIDTH]
        g0 = 3 * a + 2 * SG_WIDTH
        wga, wgb = wl[:, g0:g0 + d], wl[:, g0 + d:g0 + 2 * d]
        sgw = sg_w[l].reshape(SG_GROUPS // 2, 2 * SG_BLOCK, SG_BLOCK)
        sgb = jnp.repeat(sg_b[l].T, SG_GROUP_DIM, axis=1)
        wxkv = w_xkv[l].astype(BF16)

        bias = _bias_table(rel_bias[l])
        kmT, vm = _mem_kv(mem, row(norm_mem_g[l]), wxkv[:, :d].T, wxkv[:, d:])
        q, kT, v, ga, mb = _inproj(
            xf, row(norm_mix_g[l]), wq, wk.T, wv, wu, wvs, wga, wgb, avg,
            row(sg_ln_g[l]), row(sg_ln_b[l]), sgw, sgb, w_branch_sg[l].astype(BF16))
        xf = _attn(xf, q, kT, v, ga, mb, bias, w_branch_att[l].astype(BF16),
                   w_out[l].astype(BF16), row(norm_xattn_g[l]), w_xq[l].astype(BF16),
                   kmT, vm, w_xo[l].astype(BF16), s)
        assert depth == 1
        xf = _ffn(xf, row(norm_ffn_g[l]), w_ffn_in[l].astype(BF16), w_ffn_out[l].astype(BF16),
                  row(norm_final_g))
    return xf.reshape(b, s, d)
```

```python
import functools

import jax
import jax.numpy as jnp
import numpy as np
from jax import lax
from jax.experimental import pallas as pl
from jax.experimental.pallas import tpu as pltpu

CHUNK = 64
N_PREV_CHUNKS = 8
ATT_HEADS = 8
ATT_HEAD_DIM = 64
ATT_WIDTH = ATT_HEADS * ATT_HEAD_DIM
REL_CLIP = 128
SG_BLOCK = 128
SG_GROUPS = 8
SG_GROUP_DIM = 64
SG_WIDTH = SG_GROUPS * SG_GROUP_DIM
XATT_HEADS = 4
EPS = 1e-6
NEG_INF = -1e30
LOG2E = 1.4426950408889634

LANES = 128
MXU_WIDTH = 256
VMEM_LIMIT_BYTES = 56 * 1024 * 1024

TOKEN_TILE = 512
INPROJ_TILE = 1024
FFN_TILE = 1024
Q_SUB = 2 * CHUNK
PREV_KEYS = N_PREV_CHUNKS * CHUNK
WINDOW = PREV_KEYS + Q_SUB
FF_CHUNK = 256
SCORE_LOOKAHEAD = 2

F32 = jnp.float32
BF16 = jnp.bfloat16
_NT = (((1,), (1,)), ((), ()))


def _dot(a, b):
    return jnp.dot(a, b, preferred_element_type=F32)


def _dot_nt(a, b):
    return lax.dot_general(a, b, _NT, preferred_element_type=F32)


def _rmsnorm(x, g):
    return x * lax.rsqrt(jnp.mean(x * x, axis=-1, keepdims=True) + EPS) * g


def _const_spec(shape):
    zeros = (0,) * len(shape)
    return pl.BlockSpec(shape, lambda *_: zeros, pipeline_mode=pl.Buffered(1))


def _params():
    return pltpu.CompilerParams(dimension_semantics=("arbitrary",),
                                vmem_limit_bytes=VMEM_LIMIT_BYTES)


def _bias_kernel(rb_ref, o_ref):
    width = WINDOW + Q_SUB
    c = lax.broadcasted_iota(jnp.int32, (1, width), 1)
    idx = jnp.clip(WINDOW - c, -REL_CLIP, REL_CLIP) + REL_CLIP
    diag = jnp.zeros((ATT_HEADS, width), F32)
    for j in range(2 * REL_CLIP + 1):
        diag = jnp.where(idx == j, rb_ref[:, j:j + 1], diag)
    diag = diag * LOG2E

    row = lax.broadcasted_iota(jnp.int32, (Q_SUB, width), 0)
    qc = lax.broadcasted_iota(jnp.int32, (Q_SUB, WINDOW), 0) // CHUNK
    kc = lax.broadcasted_iota(jnp.int32, (Q_SUB, WINDOW), 1) // CHUNK
    valid = (kc >= qc) & (kc <= qc + N_PREV_CHUNKS)
    for h in range(ATT_HEADS):
        t = jnp.broadcast_to(diag[h:h + 1, :], (Q_SUB, width))
        for bit in range(Q_SUB.bit_length() - 1):
            t = jnp.where(((row >> bit) & 1) == 1, pltpu.roll(t, 1 << bit, axis=1), t)
        o_ref[h] = jnp.where(valid, t[:, Q_SUB:], NEG_INF)
    o_ref[ATT_HEADS] = jnp.full((Q_SUB, WINDOW), NEG_INF, F32)


def _bias_table(rel_bias):
    return pl.pallas_call(
        _bias_kernel,
        out_shape=jax.ShapeDtypeStruct((ATT_HEADS + 1, Q_SUB, WINDOW), F32),
        in_specs=[pl.BlockSpec(memory_space=pltpu.VMEM)],
        out_specs=pl.BlockSpec(memory_space=pltpu.VMEM),
        name="bias_table",
    )(rel_bias)


def _memkv_kernel(mem_ref, g_ref, wkT_ref, wv_ref, kT_ref, v_ref):
    mn = _rmsnorm(mem_ref[0], g_ref[...]).astype(BF16)
    kT_ref[0] = _dot_nt(wkT_ref[...], mn).astype(BF16)
    v_ref[0] = _dot(mn, wv_ref[...]).astype(BF16)


def _mem_kv(mem, g, wkT, wv):
    b, m, d = mem.shape
    return pl.pallas_call(
        _memkv_kernel,
        grid=(b,),
        out_shape=(jax.ShapeDtypeStruct((b, d, m), BF16), jax.ShapeDtypeStruct((b, m, d), BF16)),
        in_specs=[pl.BlockSpec((1, m, d), lambda i: (i, 0, 0)),
                  _const_spec((1, d)), _const_spec((d, d)), _const_spec((d, d))],
        out_specs=(pl.BlockSpec((1, d, m), lambda i: (i, 0, 0)),
                   pl.BlockSpec((1, m, d), lambda i: (i, 0, 0))),
        compiler_params=_params(),
        name="mem_kv",
    )(mem, g, wkT, wv)


def _inproj_kernel(x_ref, g_ref, wq_ref, wkT_ref, wv_ref, wu_ref, wvs_ref, wga_ref, wgb_ref,
                   avg_ref, lng_ref, lnb_ref, sgw_ref, sgb_ref, wbsg_ref,
                   q_ref, kT_ref, v_ref, ga_ref, mb_ref, wm_scr, ysg_scr):
    pair_rows = 2 * SG_BLOCK

    @pl.when(pl.program_id(0) == 0)
    def _():
        t = lax.broadcasted_iota(jnp.int32, (pair_rows, SG_BLOCK), 0) % SG_BLOCK
        s = lax.broadcasted_iota(jnp.int32, (pair_rows, SG_BLOCK), 1)
        mask = (s // CHUNK) <= (t // CHUNK)
        for j in range(SG_GROUPS // 2):
            wm_scr[j] = jnp.where(mask, sgw_ref[j], 0.0).astype(BF16)

    hb = _rmsnorm(x_ref[...], g_ref[...]).astype(BF16)
    vs_raw = _dot(hb, wvs_ref[...])
    u_raw = _dot(hb, wu_ref[...])
    q_ref[...] = (_dot(hb, wq_ref[...]) * (ATT_HEAD_DIM ** -0.5 * LOG2E)).astype(BF16)
    vs = jax.nn.gelu(vs_raw)

    def group_mean(t):
        tb = t.astype(BF16)
        w = avg_ref.shape[0]
        return jnp.concatenate([_dot(tb[:, c:c + w], avg_ref[...])
                                for c in range(0, SG_WIDTH, w)], axis=1)

    mean = group_mean(vs)
    kT_ref[...] = _dot_nt(wkT_ref[...], hb).astype(BF16)
    dev = vs - mean
    var = group_mean(dev * dev)
    v_ref[...] = _dot(hb, wv_ref[...]).astype(BF16)
    ga_raw = _dot(hb, wga_ref[...])
    vln = (dev * lax.rsqrt(var + EPS) * lng_ref[...] + lnb_ref[...]).astype(BF16)
    u = jax.nn.gelu(u_raw)

    lane = lax.broadcasted_iota(jnp.int32, (SG_BLOCK, LANES), 1)
    for n in range(x_ref.shape[0] // SG_BLOCK):
        rows = slice(n * SG_BLOCK, (n + 1) * SG_BLOCK)
        for j in range(SG_GROUPS // 2):
            cols = slice(j * LANES, (j + 1) * LANES)
            r = _dot(wm_scr[j], vln[rows, cols])
            sv = jnp.where(lane < SG_GROUP_DIM, r[:SG_BLOCK], r[SG_BLOCK:]) + sgb_ref[:, cols]
            ysg_scr[rows, cols] = (u[rows, cols] * sv).astype(BF16)

    gate_b = jax.nn.sigmoid(_dot(hb, wgb_ref[...]))
    mb_ref[...] = (gate_b * _dot(ysg_scr[...], wbsg_ref[...])).astype(BF16)
    ga_ref[...] = jax.nn.sigmoid(ga_raw).astype(BF16)


def _inproj(x, g, wq, wkT, wv, wu, wvs, wga, wgb, avg, lng, lnb, sgw, sgb, wbsg):
    n, d = x.shape
    tm = INPROJ_TILE
    assert tm % SG_BLOCK == 0
    tile = lambda w: pl.BlockSpec((tm, w), lambda i: (i, 0))
    return pl.pallas_call(
        _inproj_kernel,
        grid=(n // tm,),
        out_shape=(jax.ShapeDtypeStruct((n, ATT_WIDTH), BF16),
                   jax.ShapeDtypeStruct((ATT_WIDTH, n), BF16),
                   jax.ShapeDtypeStruct((n, ATT_WIDTH), BF16),
                   jax.ShapeDtypeStruct((n, d), BF16),
                   jax.ShapeDtypeStruct((n, d), BF16)),
        in_specs=[tile(d), _const_spec((1, d)),
                  _const_spec(wq.shape), _const_spec(wkT.shape), _const_spec(wv.shape),
                  _const_spec(wu.shape), _const_spec(wvs.shape), _const_spec(wga.shape),
                  _const_spec(wgb.shape), _const_spec(avg.shape), _const_spec(lng.shape),
                  _const_spec(lnb.shape), _const_spec(sgw.shape), _const_spec(sgb.shape),
                  _const_spec(wbsg.shape)],
        out_specs=(tile(ATT_WIDTH), pl.BlockSpec((ATT_WIDTH, tm), lambda i: (0, i)),
                   tile(ATT_WIDTH), tile(d), tile(d)),
        scratch_shapes=[pltpu.VMEM((SG_GROUPS // 2, 2 * SG_BLOCK, SG_BLOCK), BF16),
                        pltpu.VMEM((tm, SG_WIDTH), BF16)],
        compiler_params=_params(),
        name="inproj_sgu",
    )(x, g, wq, wkT, wv, wu, wvs, wga, wgb, avg, lng, lnb, sgw, sgb, wbsg)


def _attn_kernel(tiles_per_seq, x_ref, q_ref, kTp_ref, kTc_ref, vp_ref, vc_ref, ga_ref, mb_ref,
                 bias_ref, wbatt_ref, wout_ref, gx_ref, wxq_ref, kmT_ref, vm_ref, wxo_ref,
                 o_ref, kT_scr, v_scr, yatt_scr, xo_scr):
    tm = x_ref.shape[0]
    d = x_ref.shape[1]
    first_tile = (pl.program_id(0) % tiles_per_seq) == 0

    kT_scr[:, :tm] = kTp_ref[...]
    kT_scr[:, tm:] = kTc_ref[...]
    vcat = jnp.concatenate([vp_ref[...], vc_ref[...]], axis=0)
    low_half = (lax.broadcasted_iota(jnp.int32, vcat.shape, 1) % LANES) < ATT_HEAD_DIM
    v_scr[0] = jnp.where(low_half, vcat, jnp.ones_like(vcat))
    v_scr[1] = jnp.where(low_half, jnp.ones_like(vcat), vcat)

    lane = lax.broadcasted_iota(jnp.int32, (Q_SUB, LANES), 1)
    n_blocks = WINDOW // LANES

    def window(j, h):
        q0 = j * Q_SUB
        cols = slice((h // 2) * LANES, (h // 2 + 1) * LANES)
        return q0, tm + q0 - PREV_KEYS, cols

    def scores(j, h):
        q0, w0, cols = window(j, h)
        qp = q_ref[q0:q0 + Q_SUB, cols]
        qz = jnp.where((lane < ATT_HEAD_DIM) == (h % 2 == 0), qp, jnp.zeros_like(qp))
        s = _dot(qz, kT_scr[cols, w0:w0 + WINDOW])
        blocks = []
        for b in range(n_blocks):
            before_start = (b + 1) * LANES <= PREV_KEYS - q0
            entry = jnp.where(first_tile, ATT_HEADS, h) if before_start else h
            blk = slice(b * LANES, (b + 1) * LANES)
            blocks.append(s[:, blk] + bias_ref[entry, :, blk])
        return jnp.concatenate(blocks, axis=1)

    def attend(j, h, s):
        _, w0, cols = window(j, h)
        p = jnp.exp2(s - jnp.max(s, axis=-1, keepdims=True)).astype(BF16)
        o = _dot(p, v_scr[h % 2, w0:w0 + WINDOW, cols])
        return o / pltpu.roll(o, ATT_HEAD_DIM, axis=1)

    steps = [(j, h) for j in range(tm // Q_SUB) for h in range(ATT_HEADS)]
    pending = [scores(*st) for st in steps[:SCORE_LOOKAHEAD]]
    for k, (j, h) in enumerate(steps):
        if k + SCORE_LOOKAHEAD < len(steps):
            pending.append(scores(*steps[k + SCORE_LOOKAHEAD]))
        o = attend(j, h, pending.pop(0))
        if h % 2 == 0:
            o_even = o
        else:
            q0, _, cols = window(j, h)
            y = jnp.where(lane < ATT_HEAD_DIM, o_even, o)
            yatt_scr[q0:q0 + Q_SUB, cols] = y.astype(BF16)

    att = _dot(yatt_scr[...], wbatt_ref[...])
    merged = ga_ref[...].astype(F32) * att + mb_ref[...].astype(F32)
    x1 = x_ref[...] + _dot(merged.astype(BF16), wout_ref[...])

    hx = _rmsnorm(x1, gx_ref[...]).astype(BF16)
    xd = d // XATT_HEADS
    qx = (_dot(hx, wxq_ref[...]) * (xd ** -0.5)).astype(BF16)
    for h in range(XATT_HEADS):
        cols = slice(h * xd, (h + 1) * xd)
        s = _dot(qx[:, cols], kmT_ref[0, cols, :])
        e = jnp.exp(s - jnp.max(s, axis=-1, keepdims=True))
        den = jnp.sum(e, axis=-1, keepdims=True)
        xo_scr[:, cols] = (_dot(e.astype(BF16), vm_ref[0, :, cols]) / den).astype(BF16)
    o_ref[...] = x1 + _dot(xo_scr[...], wxo_ref[...])


def _attn(x, q, kT, v, ga, mb, bias, wbatt, wout, gx, wxq, kmT, vm, wxo, seq):
    n, d = x.shape
    tm = TOKEN_TILE
    tps = seq // tm
    mem_len = vm.shape[1]
    tile = lambda w: pl.BlockSpec((tm, w), lambda i: (i, 0))
    prev = lambda i: jnp.maximum(i - 1, 0)
    return pl.pallas_call(
        functools.partial(_attn_kernel, tps),
        grid=(n // tm,),
        out_shape=jax.ShapeDtypeStruct((n, d), F32),
        in_specs=[tile(d), tile(ATT_WIDTH),
                  pl.BlockSpec((ATT_WIDTH, tm), lambda i: (0, prev(i))),
                  pl.BlockSpec((ATT_WIDTH, tm), lambda i: (0, i)),
                  pl.BlockSpec((tm, ATT_WIDTH), lambda i: (prev(i), 0)),
                  tile(ATT_WIDTH), tile(d), tile(d),
                  _const_spec(bias.shape), _const_spec(wbatt.shape), _const_spec(wout.shape),
                  _const_spec((1, d)), _const_spec(wxq.shape),
                  pl.BlockSpec((1, d, mem_len), lambda i: (i // tps, 0, 0)),
                  pl.BlockSpec((1, mem_len, d), lambda i: (i // tps, 0, 0)),
                  _const_spec(wxo.shape)],
        out_specs=tile(d),
        scratch_shapes=[pltpu.VMEM((ATT_WIDTH, 2 * tm), BF16),
                        pltpu.VMEM((2, 2 * tm, ATT_WIDTH), BF16),
                        pltpu.VMEM((tm, ATT_WIDTH), BF16),
                        pltpu.VMEM((tm, d), BF16)],
        compiler_params=_params(),
        name="attn_merge_xattn",
    )(x, q, kT, kT, v, v, ga, mb, bias, wbatt, wout, gx, wxq, kmT, vm, wxo)


def _ffn_kernel(x_ref, g_ref, wfi_ref, wfo_ref, gfin_ref, o_ref, a_scr):
    d_ff = wfo_ref.shape[0]
    x = x_ref[...]
    hb = _rmsnorm(x, g_ref[...]).astype(BF16)
    for c in range(d_ff // FF_CHUNK):
        gate = _dot(hb, wfi_ref[:, c * FF_CHUNK:(c + 1) * FF_CHUNK])
        up = _dot(hb, wfi_ref[:, d_ff + c * FF_CHUNK:d_ff + (c + 1) * FF_CHUNK])
        a_scr[:, c * FF_CHUNK:(c + 1) * FF_CHUNK] = (jax.nn.silu(gate) * up).astype(BF16)
    x3 = x + _dot(a_scr[...], wfo_ref[...])
    o_ref[...] = _rmsnorm(x3, gfin_ref[...])


def _ffn(x, g, wfi, wfo, gfin):
    n, d = x.shape
    tm = FFN_TILE
    d_ff = wfo.shape[0]
    assert d_ff % FF_CHUNK == 0
    tile = pl.BlockSpec((tm, d), lambda i: (i, 0))
    return pl.pallas_call(
        _ffn_kernel,
        grid=(n // tm,),
        out_shape=jax.ShapeDtypeStruct((n, d), F32),
        in_specs=[tile, _const_spec((1, d)), _const_spec(wfi.shape), _const_spec(wfo.shape),
                  _const_spec((1, d))],
        out_specs=tile,
        scratch_shapes=[pltpu.VMEM((tm, d_ff), BF16)],
        compiler_params=_params(),
        name="ffn_final_norm",
    )(x, g, wfi, wfo, gfin)


def kernel(x, mem, norm_mix_g, w_in, rel_bias, sg_ln_g, sg_ln_b, sg_w, sg_b, w_branch_att,
           w_branch_sg, w_out, norm_xattn_g, norm_mem_g, w_xq, w_xkv, w_xo, norm_ffn_g,
           w_ffn_in, w_ffn_out, norm_final_g):
    b, s, d = x.shape
    depth = w_in.shape[0]
    assert s % TOKEN_TILE == 0 and TOKEN_TILE >= PREV_KEYS and TOKEN_TILE % Q_SUB == 0
    assert (b * s) % FFN_TILE == 0 and (b * s) % INPROJ_TILE == 0
    row = lambda a: a.reshape(1, -1).astype(F32)
    grp = np.arange(MXU_WIDTH) // SG_GROUP_DIM
    avg = jnp.asarray((grp[:, None] == grp[None, :]) / SG_GROUP_DIM, BF16)

    xf = x.reshape(b * s, d)
    for l in range(depth):
        wl = w_in[l].astype(BF16)
        a = ATT_WIDTH
        wq, wk, wv = wl[:, :a], wl[:, a:2 * a], wl[:, 2 * a:3 * a]
        wu, wvs = wl[:, 3 * a:3 * a + SG_WIDTH], wl[:, 3 * a + SG_WIDTH:3 * a + 2 * SG_WIDTH]
        g0 = 3 * a + 2 * SG_WIDTH
        wga, wgb = wl[:, g0:g0 + d], wl[:, g0 + d:g0 + 2 * d]
        sgw = sg_w[l].reshape(SG_GROUPS // 2, 2 * SG_BLOCK, SG_BLOCK)
        sgb = jnp.repeat(sg_b[l].T, SG_GROUP_DIM, axis=1)
        wxkv = w_xkv[l].astype(BF16)

        bias = _bias_table(rel_bias[l])
        kmT, vm = _mem_kv(mem, row(norm_mem_g[l]), wxkv[:, :d].T, wxkv[:, d:])
        q, kT, v, ga, mb = _inproj(
            xf, row(norm_mix_g[l]), wq, wk.T, wv, wu, wvs, wga, wgb, avg,
            row(sg_ln_g[l]), row(sg_ln_b[l]), sgw, sgb, w_branch_sg[l].astype(BF16))
        xf = _attn(xf, q, kT, v, ga, mb, bias, w_branch_att[l].astype(BF16),
                   w_out[l].astype(BF16), row(norm_xattn_g[l]), w_xq[l].astype(BF16),
                   kmT, vm, w_xo[l].astype(BF16), s)
        assert depth == 1
        xf = _ffn(xf, row(norm_ffn_g[l]), w_ffn_in[l].astype(BF16), w_ffn_out[l].astype(BF16),
                  row(norm_final_g))
    return xf.reshape(b, s, d)
```

```python
import functools

import jax
import jax.numpy as jnp
import numpy as np
from jax import lax
from jax.experimental import pallas as pl
from jax.experimental.pallas import tpu as pltpu

CHUNK = 64
N_PREV_CHUNKS = 8
ATT_HEADS = 8
ATT_HEAD_DIM = 64
ATT_WIDTH = ATT_HEADS * ATT_HEAD_DIM
REL_CLIP = 128
SG_BLOCK = 128
SG_GROUPS = 8
SG_GROUP_DIM = 64
SG_WIDTH = SG_GROUPS * SG_GROUP_DIM
XATT_HEADS = 4
EPS = 1e-6
NEG_INF = -1e30
LOG2E = 1.4426950408889634

LANES = 128
MXU_WIDTH = 256
VMEM_LIMIT_BYTES = 56 * 1024 * 1024

TOKEN_TILE = 512
INPROJ_TILE = 1024
FFN_TILE = 1024
FFN_SUB = 512
Q_SUB = 2 * CHUNK
PREV_KEYS = N_PREV_CHUNKS * CHUNK
WINDOW = PREV_KEYS + Q_SUB
FF_CHUNK = 256
SCORE_SUBS = 1
SCORE_LOOKAHEAD = 2

F32 = jnp.float32
BF16 = jnp.bfloat16
_NT = (((1,), (1,)), ((), ()))


def _dot(a, b):
    return jnp.dot(a, b, preferred_element_type=F32)


def _dot_nt(a, b):
    return lax.dot_general(a, b, _NT, preferred_element_type=F32)


def _rmsnorm(x, g):
    return x * lax.rsqrt(jnp.mean(x * x, axis=-1, keepdims=True) + EPS) * g


def _const_spec(shape):
    zeros = (0,) * len(shape)
    return pl.BlockSpec(shape, lambda *_: zeros, pipeline_mode=pl.Buffered(1))


def _params():
    return pltpu.CompilerParams(dimension_semantics=("arbitrary",),
                                vmem_limit_bytes=VMEM_LIMIT_BYTES)


def _bias_kernel(rb_ref, o_ref):
    width = WINDOW + Q_SUB
    c = lax.broadcasted_iota(jnp.int32, (1, width), 1)
    idx = jnp.clip(WINDOW - c, -REL_CLIP, REL_CLIP) + REL_CLIP
    diag = jnp.zeros((ATT_HEADS, width), F32)
    for j in range(2 * REL_CLIP + 1):
        diag = jnp.where(idx == j, rb_ref[:, j:j + 1], diag)
    diag = diag * LOG2E

    row = lax.broadcasted_iota(jnp.int32, (Q_SUB, width), 0)
    qc = lax.broadcasted_iota(jnp.int32, (Q_SUB, WINDOW), 0) // CHUNK
    kc = lax.broadcasted_iota(jnp.int32, (Q_SUB, WINDOW), 1) // CHUNK
    valid = (kc >= qc) & (kc <= qc + N_PREV_CHUNKS)
    for h in range(ATT_HEADS):
        t = jnp.broadcast_to(diag[h:h + 1, :], (Q_SUB, width))
        for bit in range(Q_SUB.bit_length() - 1):
            t = jnp.where(((row >> bit) & 1) == 1, pltpu.roll(t, 1 << bit, axis=1), t)
        o_ref[h] = jnp.where(valid, t[:, Q_SUB:], NEG_INF)
    o_ref[ATT_HEADS] = jnp.full((Q_SUB, WINDOW), NEG_INF, F32)


def _bias_table(rel_bias):
    return pl.pallas_call(
        _bias_kernel,
        out_shape=jax.ShapeDtypeStruct((ATT_HEADS + 1, Q_SUB, WINDOW), F32),
        in_specs=[pl.BlockSpec(memory_space=pltpu.VMEM)],
        out_specs=pl.BlockSpec(memory_space=pltpu.VMEM),
        name="bias_table",
    )(rel_bias)


def _memkv_kernel(mem_ref, g_ref, wkT_ref, wv_ref, kT_ref, v_ref):
    mn = _rmsnorm(mem_ref[0], g_ref[...]).astype(BF16)
    kT_ref[0] = _dot_nt(wkT_ref[...], mn).astype(BF16)
    v_ref[0] = _dot(mn, wv_ref[...]).astype(BF16)


def _mem_kv(mem, g, wkT, wv):
    b, m, d = mem.shape
    return pl.pallas_call(
        _memkv_kernel,
        grid=(b,),
        out_shape=(jax.ShapeDtypeStruct((b, d, m), BF16), jax.ShapeDtypeStruct((b, m, d), BF16)),
        in_specs=[pl.BlockSpec((1, m, d), lambda i: (i, 0, 0)),
                  _const_spec((1, d)), _const_spec((d, d)), _const_spec((d, d))],
        out_specs=(pl.BlockSpec((1, d, m), lambda i: (i, 0, 0)),
                   pl.BlockSpec((1, m, d), lambda i: (i, 0, 0))),
        compiler_params=_params(),
        name="mem_kv",
    )(mem, g, wkT, wv)


def _inproj_kernel(x_ref, g_ref, wq_ref, wkT_ref, wv_ref, wu_ref, wvs_ref, wga_ref, wgb_ref,
                   avg_ref, lng_ref, lnb_ref, sgw_ref, sgb_ref, wbsg_ref,
                   q_ref, kT_ref, v_ref, ga_ref, mb_ref, wm_scr, ysg_scr):
    pair_rows = 2 * SG_BLOCK

    @pl.when(pl.program_id(0) == 0)
    def _():
        t = lax.broadcasted_iota(jnp.int32, (pair_rows, SG_BLOCK), 0) % SG_BLOCK
        s = lax.broadcasted_iota(jnp.int32, (pair_rows, SG_BLOCK), 1)
        mask = (s // CHUNK) <= (t // CHUNK)
        for j in range(SG_GROUPS // 2):
            wm_scr[j] = jnp.where(mask, sgw_ref[j], 0.0).astype(BF16)

    hb = _rmsnorm(x_ref[...], g_ref[...]).astype(BF16)
    vs_raw = _dot(hb, wvs_ref[...])
    u_raw = _dot(hb, wu_ref[...])
    q_ref[...] = (_dot(hb, wq_ref[...]) * (ATT_HEAD_DIM ** -0.5 * LOG2E)).astype(BF16)
    vs = jax.nn.gelu(vs_raw)

    def group_mean(t):
        tb = t.astype(BF16)
        w = avg_ref.shape[0]
        return jnp.concatenate([_dot(tb[:, c:c + w], avg_ref[...])
                                for c in range(0, SG_WIDTH, w)], axis=1)

    mean = group_mean(vs)
    kT_ref[...] = _dot_nt(wkT_ref[...], hb).astype(BF16)
    dev = vs - mean
    var = group_mean(dev * dev)
    v_ref[...] = _dot(hb, wv_ref[...]).astype(BF16)
    ga_raw = _dot(hb, wga_ref[...])
    vln = (dev * lax.rsqrt(var + EPS) * lng_ref[...] + lnb_ref[...]).astype(BF16)
    u = jax.nn.gelu(u_raw)

    lane = lax.broadcasted_iota(jnp.int32, (SG_BLOCK, LANES), 1)
    for n in range(x_ref.shape[0] // SG_BLOCK):
        rows = slice(n * SG_BLOCK, (n + 1) * SG_BLOCK)
        for j in range(SG_GROUPS // 2):
            cols = slice(j * LANES, (j + 1) * LANES)
            r = _dot(wm_scr[j], vln[rows, cols])
            sv = jnp.where(lane < SG_GROUP_DIM, r[:SG_BLOCK], r[SG_BLOCK:]) + sgb_ref[:, cols]
            ysg_scr[rows, cols] = (u[rows, cols] * sv).astype(BF16)

    gate_b = jax.nn.sigmoid(_dot(hb, wgb_ref[...]))
    mb_ref[...] = (gate_b * _dot(ysg_scr[...], wbsg_ref[...])).astype(BF16)
    ga_ref[...] = jax.nn.sigmoid(ga_raw).astype(BF16)


def _inproj(x, g, wq, wkT, wv, wu, wvs, wga, wgb, avg, lng, lnb, sgw, sgb, wbsg):
    n, d = x.shape
    tm = INPROJ_TILE
    assert tm % SG_BLOCK == 0
    tile = lambda w: pl.BlockSpec((tm, w), lambda i: (i, 0))
    return pl.pallas_call(
        _inproj_kernel,
        grid=(n // tm,),
        out_shape=(jax.ShapeDtypeStruct((n, ATT_WIDTH), BF16),
                   jax.ShapeDtypeStruct((ATT_WIDTH, n), BF16),
                   jax.ShapeDtypeStruct((n, ATT_WIDTH), BF16),
                   jax.ShapeDtypeStruct((n, d), BF16),
                   jax.ShapeDtypeStruct((n, d), BF16)),
        in_specs=[tile(d), _const_spec((1, d)),
                  _const_spec(wq.shape), _const_spec(wkT.shape), _const_spec(wv.shape),
                  _const_spec(wu.shape), _const_spec(wvs.shape), _const_spec(wga.shape),
                  _const_spec(wgb.shape), _const_spec(avg.shape), _const_spec(lng.shape),
                  _const_spec(lnb.shape), _const_spec(sgw.shape), _const_spec(sgb.shape),
                  _const_spec(wbsg.shape)],
        out_specs=(tile(ATT_WIDTH), pl.BlockSpec((ATT_WIDTH, tm), lambda i: (0, i)),
                   tile(ATT_WIDTH), tile(d), tile(d)),
        scratch_shapes=[pltpu.VMEM((SG_GROUPS // 2, 2 * SG_BLOCK, SG_BLOCK), BF16),
                        pltpu.VMEM((tm, SG_WIDTH), BF16)],
        compiler_params=_params(),
        name="inproj_sgu",
    )(x, g, wq, wkT, wv, wu, wvs, wga, wgb, avg, lng, lnb, sgw, sgb, wbsg)


def _attn_kernel(tiles_per_seq, x_ref, q_ref, kTp_ref, kTc_ref, vp_ref, vc_ref, ga_ref, mb_ref,
                 bias_ref, wbatt_ref, wout_ref, gx_ref, wxq_ref, kmT_ref, vm_ref, wxo_ref,
                 o_ref, kT_scr, v_scr, yatt_scr, xo_scr):
    tm = x_ref.shape[0]
    d = x_ref.shape[1]
    first_tile = (pl.program_id(0) % tiles_per_seq) == 0

    kT_scr[:, :tm] = kTp_ref[...]
    kT_scr[:, tm:] = kTc_ref[...]
    vcat = jnp.concatenate([vp_ref[...], vc_ref[...]], axis=0)
    low_half = (lax.broadcasted_iota(jnp.int32, vcat.shape, 1) % LANES) < ATT_HEAD_DIM
    v_scr[0] = jnp.where(low_half, vcat, jnp.ones_like(vcat))
    v_scr[1] = jnp.where(low_half, jnp.ones_like(vcat), vcat)

    lane = lax.broadcasted_iota(jnp.int32, (Q_SUB, LANES), 1)
    n_blocks = WINDOW // LANES

    def window(j, h):
        q0 = j * Q_SUB
        cols = slice((h // 2) * LANES, (h // 2 + 1) * LANES)
        return q0, tm + q0 - PREV_KEYS, cols

    def scores(g, p):
        j0 = g * SCORE_SUBS
        q0, w0, cols = window(j0, 2 * p)
        parts = []
        for t in range(SCORE_SUBS):
            qp = q_ref[q0 + t * Q_SUB:q0 + (t + 1) * Q_SUB, cols]
            zero = jnp.zeros_like(qp)
            parts += [jnp.where(lane < ATT_HEAD_DIM, qp, zero),
                      jnp.where(lane < ATT_HEAD_DIM, zero, qp)]
        s = _dot(jnp.concatenate(parts, axis=0),
                 kT_scr[cols, w0:w0 + WINDOW + (SCORE_SUBS - 1) * Q_SUB])
        out = {}
        for t in range(SCORE_SUBS):
            for hh in range(2):
                h = 2 * p + hh
                r0 = (2 * t + hh) * Q_SUB
                blocks = []
                for b in range(n_blocks):
                    before_start = (b + 1) * LANES <= PREV_KEYS - (q0 + t * Q_SUB)
                    entry = jnp.where(first_tile, ATT_HEADS, h) if before_start else h
                    blk = slice(b * LANES, (b + 1) * LANES)
                    sblk = slice(t * Q_SUB + b * LANES, t * Q_SUB + (b + 1) * LANES)
                    blocks.append(s[r0:r0 + Q_SUB, sblk] + bias_ref[entry, :, blk])
                out[j0 + t, h] = jnp.concatenate(blocks, axis=1)
        return out

    def attend(j, h, s):
        _, w0, cols = window(j, h)
        p = jnp.exp2(s - jnp.max(s, axis=-1, keepdims=True)).astype(BF16)
        o = _dot(p, v_scr[h % 2, w0:w0 + WINDOW, cols])
        return o / pltpu.roll(o, ATT_HEAD_DIM, axis=1)

    groups = [(g, p) for g in range(tm // (Q_SUB * SCORE_SUBS)) for p in range(ATT_HEADS // 2)]
    pending = [scores(*gp) for gp in groups[:SCORE_LOOKAHEAD]]
    for k, (g, p) in enumerate(groups):
        if k + SCORE_LOOKAHEAD < len(groups):
            pending.append(scores(*groups[k + SCORE_LOOKAHEAD]))
        s = pending.pop(0)
        for j in range(g * SCORE_SUBS, (g + 1) * SCORE_SUBS):
            o_even = attend(j, 2 * p, s.pop((j, 2 * p)))
            o_odd = attend(j, 2 * p + 1, s.pop((j, 2 * p + 1)))
            q0, _, cols = window(j, 2 * p)
            y = jnp.where(lane < ATT_HEAD_DIM, o_even, o_odd)
            yatt_scr[q0:q0 + Q_SUB, cols] = y.astype(BF16)

    att = _dot(yatt_scr[...], wbatt_ref[...])
    merged = ga_ref[...].astype(F32) * att + mb_ref[...].astype(F32)
    x1 = x_ref[...] + _dot(merged.astype(BF16), wout_ref[...])

    hx = _rmsnorm(x1, gx_ref[...]).astype(BF16)
    xd = d // XATT_HEADS
    qx = (_dot(hx, wxq_ref[...]) * (xd ** -0.5)).astype(BF16)
    for h in range(XATT_HEADS):
        cols = slice(h * xd, (h + 1) * xd)
        s = _dot(qx[:, cols], kmT_ref[0, cols, :])
        e = jnp.exp(s - jnp.max(s, axis=-1, keepdims=True))
        den = jnp.sum(e, axis=-1, keepdims=True)
        xo_scr[:, cols] = (_dot(e.astype(BF16), vm_ref[0, :, cols]) / den).astype(BF16)
    o_ref[...] = x1 + _dot(xo_scr[...], wxo_ref[...])


def _attn(x, q, kT, v, ga, mb, bias, wbatt, wout, gx, wxq, kmT, vm, wxo, seq):
    n, d = x.shape
    tm = TOKEN_TILE
    tps = seq // tm
    mem_len = vm.shape[1]
    tile = lambda w: pl.BlockSpec((tm, w), lambda i: (i, 0))
    prev = lambda i: jnp.maximum(i - 1, 0)
    return pl.pallas_call(
        functools.partial(_attn_kernel, tps),
        grid=(n // tm,),
        out_shape=jax.ShapeDtypeStruct((n, d), F32),
        in_specs=[tile(d), tile(ATT_WIDTH),
                  pl.BlockSpec((ATT_WIDTH, tm), lambda i: (0, prev(i))),
                  pl.BlockSpec((ATT_WIDTH, tm), lambda i: (0, i)),
                  pl.BlockSpec((tm, ATT_WIDTH), lambda i: (prev(i), 0)),
                  tile(ATT_WIDTH), tile(d), tile(d),
                  _const_spec(bias.shape), _const_spec(wbatt.shape), _const_spec(wout.shape),
                  _const_spec((1, d)), _const_spec(wxq.shape),
                  pl.BlockSpec((1, d, mem_len), lambda i: (i // tps, 0, 0)),
                  pl.BlockSpec((1, mem_len, d), lambda i: (i // tps, 0, 0)),
                  _const_spec(wxo.shape)],
        out_specs=tile(d),
        scratch_shapes=[pltpu.VMEM((ATT_WIDTH, 2 * tm), BF16),
                        pltpu.VMEM((2, 2 * tm, ATT_WIDTH), BF16),
                        pltpu.VMEM((tm, ATT_WIDTH), BF16),
                        pltpu.VMEM((tm, d), BF16)],
        compiler_params=_params(),
        name="attn_merge_xattn",
    )(x, q, kT, kT, v, v, ga, mb, bias, wbatt, wout, gx, wxq, kmT, vm, wxo)


def _ffn_kernel(x_ref, g_ref, wfi_ref, wfo_ref, gfin_ref, o_ref, a_scr):
    d_ff = wfo_ref.shape[0]
    subs = [slice(r, r + FFN_SUB) for r in range(0, x_ref.shape[0], FFN_SUB)]
    hb_next = _rmsnorm(x_ref[subs[0], :], g_ref[...]).astype(BF16)
    for r, rows in enumerate(subs):
        hb = hb_next
        for c in range(d_ff // FF_CHUNK):
            gate = _dot(hb, wfi_ref[:, c * FF_CHUNK:(c + 1) * FF_CHUNK])
            up = _dot(hb, wfi_ref[:, d_ff + c * FF_CHUNK:d_ff + (c + 1) * FF_CHUNK])
            a_scr[rows, c * FF_CHUNK:(c + 1) * FF_CHUNK] = (jax.nn.silu(gate) * up).astype(BF16)
            if c == 0 and r + 1 < len(subs):
                hb_next = _rmsnorm(x_ref[subs[r + 1], :], g_ref[...]).astype(BF16)
        x3 = x_ref[rows, :] + _dot(a_scr[rows, :], wfo_ref[...])
        o_ref[rows, :] = _rmsnorm(x3, gfin_ref[...])


def _ffn(x, g, wfi, wfo, gfin):
    n, d = x.shape
    tm = FFN_TILE
    d_ff = wfo.shape[0]
    assert d_ff % FF_CHUNK == 0
    tile = pl.BlockSpec((tm, d), lambda i: (i, 0))
    return pl.pallas_call(
        _ffn_kernel,
        grid=(n // tm,),
        out_shape=jax.ShapeDtypeStruct((n, d), F32),
        in_specs=[tile, _const_spec((1, d)), _const_spec(wfi.shape), _const_spec(wfo.shape),
                  _const_spec((1, d))],
        out_specs=tile,
        scratch_shapes=[pltpu.VMEM((tm, d_ff), BF16)],
        compiler_params=_params(),
        name="ffn_final_norm",
    )(x, g, wfi, wfo, gfin)


def kernel(x, mem, norm_mix_g, w_in, rel_bias, sg_ln_g, sg_ln_b, sg_w, sg_b, w_branch_att,
           w_branch_sg, w_out, norm_xattn_g, norm_mem_g, w_xq, w_xkv, w_xo, norm_ffn_g,
           w_ffn_in, w_ffn_out, norm_final_g):
    b, s, d = x.shape
    depth = w_in.shape[0]
    assert s % TOKEN_TILE == 0 and TOKEN_TILE >= PREV_KEYS and TOKEN_TILE % Q_SUB == 0
    assert (b * s) % FFN_TILE == 0 and (b * s) % INPROJ_TILE == 0
    row = lambda a: a.reshape(1, -1).astype(F32)
    grp = np.arange(MXU_WIDTH) // SG_GROUP_DIM
    avg = jnp.asarray((grp[:, None] == grp[None, :]) / SG_GROUP_DIM, BF16)

    xf = x.reshape(b * s, d)
    for l in range(depth):
        wl = w_in[l].astype(BF16)
        a = ATT_WIDTH
        wq, wk, wv = wl[:, :a], wl[:, a:2 * a], wl[:, 2 * a:3 * a]
        wu, wvs = wl[:, 3 * a:3 * a + SG_WIDTH], wl[:, 3 * a + SG_WIDTH:3 * a + 2 * SG_WIDTH]
        g0 = 3 * a + 2 * SG_WIDTH
        wga, wgb = wl[:, g0:g0 + d], wl[:, g0 + d:g0 + 2 * d]
        sgw = sg_w[l].reshape(SG_GROUPS // 2, 2 * SG_BLOCK, SG_BLOCK)
        sgb = jnp.repeat(sg_b[l].T, SG_GROUP_DIM, axis=1)
        wxkv = w_xkv[l].astype(BF16)

        bias = _bias_table(rel_bias[l])
        kmT, vm = _mem_kv(mem, row(norm_mem_g[l]), wxkv[:, :d].T, wxkv[:, d:])
        q, kT, v, ga, mb = _inproj(
            xf, row(norm_mix_g[l]), wq, wk.T, wv, wu, wvs, wga, wgb, avg,
            row(sg_ln_g[l]), row(sg_ln_b[l]), sgw, sgb, w_branch_sg[l].astype(BF16))
        xf = _attn(xf, q, kT, v, ga, mb, bias, w_branch_att[l].astype(BF16),
                   w_out[l].astype(BF16), row(norm_xattn_g[l]), w_xq[l].astype(BF16),
                   kmT, vm, w_xo[l].astype(BF16), s)
        assert depth == 1
        xf = _ffn(xf, row(norm_ffn_g[l]), w_ffn_in[l].astype(BF16), w_ffn_out[l].astype(BF16),
                  row(norm_final_g))
    return xf.reshape(b, s, d)
```

```python
import functools

import jax
import jax.numpy as jnp
import numpy as np
from jax import lax
from jax.experimental import pallas as pl
from jax.experimental.pallas import tpu as pltpu

CHUNK = 64
N_PREV_CHUNKS = 8
ATT_HEADS = 8
ATT_HEAD_DIM = 64
ATT_WIDTH = ATT_HEADS * ATT_HEAD_DIM
REL_CLIP = 128
SG_BLOCK = 128
SG_GROUPS = 8
SG_GROUP_DIM = 64
SG_WIDTH = SG_GROUPS * SG_GROUP_DIM
XATT_HEADS = 4
EPS = 1e-6
NEG_INF = -1e30
LOG2E = 1.4426950408889634

LANES = 128
MXU_WIDTH = 256
VMEM_LIMIT_BYTES = 56 * 1024 * 1024

TOKEN_TILE = 512
INPROJ_TILE = 1024
FFN_TILE = 1024
FFN_SUB = 512
Q_SUB = 2 * CHUNK
PREV_KEYS = N_PREV_CHUNKS * CHUNK
WINDOW = PREV_KEYS + Q_SUB
FF_CHUNK = 256
SCORE_LOOKAHEAD = 2

F32 = jnp.float32
BF16 = jnp.bfloat16
_NT = (((1,), (1,)), ((), ()))


def _dot(a, b):
    return jnp.dot(a, b, preferred_element_type=F32)


def _dot_nt(a, b):
    return lax.dot_general(a, b, _NT, preferred_element_type=F32)


def _rmsnorm(x, g):
    return x * lax.rsqrt(jnp.mean(x * x, axis=-1, keepdims=True) + EPS) * g


def _const_spec(shape):
    zeros = (0,) * len(shape)
    return pl.BlockSpec(shape, lambda *_: zeros, pipeline_mode=pl.Buffered(1))


def _params():
    return pltpu.CompilerParams(dimension_semantics=("arbitrary",),
                                vmem_limit_bytes=VMEM_LIMIT_BYTES)


def _bias_kernel(rb_ref, o_ref):
    near_cols = Q_SUB + REL_CLIP
    far_cols = WINDOW - near_cols
    width = near_cols + Q_SUB
    c = lax.broadcasted_iota(jnp.int32, (1, width), 1)
    idx = jnp.clip(near_cols - c, -REL_CLIP, REL_CLIP) + REL_CLIP
    diag = jnp.zeros((ATT_HEADS, width), F32)
    for j in range(2 * REL_CLIP + 1):
        diag = jnp.where(idx == j, rb_ref[:, j:j + 1], diag)
    diag = diag * LOG2E

    row = lax.broadcasted_iota(jnp.int32, (Q_SUB, width), 0)
    qc = lax.broadcasted_iota(jnp.int32, (Q_SUB, WINDOW), 0) // CHUNK
    kc = lax.broadcasted_iota(jnp.int32, (Q_SUB, WINDOW), 1) // CHUNK
    valid = (kc >= qc) & (kc <= qc + N_PREV_CHUNKS)
    for h in range(ATT_HEADS):
        t = jnp.broadcast_to(diag[h:h + 1, :], (Q_SUB, width))
        for bit in range(Q_SUB.bit_length() - 1):
            t = jnp.where(((row >> bit) & 1) == 1, pltpu.roll(t, 1 << bit, axis=1), t)
        far = jnp.broadcast_to(diag[h:h + 1, 0:1], (Q_SUB, far_cols))
        o_ref[h] = jnp.where(valid, jnp.concatenate([far, t[:, Q_SUB:]], axis=1), NEG_INF)
    o_ref[ATT_HEADS] = jnp.full((Q_SUB, WINDOW), NEG_INF, F32)


def _bias_table(rel_bias):
    return pl.pallas_call(
        _bias_kernel,
        out_shape=jax.ShapeDtypeStruct((ATT_HEADS + 1, Q_SUB, WINDOW), F32),
        in_specs=[pl.BlockSpec(memory_space=pltpu.VMEM)],
        out_specs=pl.BlockSpec(memory_space=pltpu.VMEM),
        name="bias_table",
    )(rel_bias)


def _memkv_kernel(mem_ref, g_ref, wkT_ref, wkv_ref, kT_ref, v_ref):
    d, m = kT_ref.shape[1:]
    mn = _rmsnorm(mem_ref[...], g_ref[...]).astype(BF16)
    v_ref[...] = _dot(mn, wkv_ref[:, d:]).astype(BF16)
    for b in range(kT_ref.shape[0]):
        kT_ref[b] = _dot_nt(wkT_ref[...], mn[b * m:(b + 1) * m]).astype(BF16)


def _mem_kv(mem, g, wkT, wkv):
    b, m, d = mem.shape
    vmem = pl.BlockSpec(memory_space=pltpu.VMEM)
    kT, v = pl.pallas_call(
        _memkv_kernel,
        out_shape=(jax.ShapeDtypeStruct((b, d, m), BF16), jax.ShapeDtypeStruct((b * m, d), BF16)),
        in_specs=[vmem, vmem, vmem, vmem],
        out_specs=(vmem, vmem),
        compiler_params=pltpu.CompilerParams(vmem_limit_bytes=VMEM_LIMIT_BYTES),
        name="mem_kv",
    )(mem.reshape(b * m, d), g, wkT, wkv)
    return kT, v.reshape(b, m, d)


def _inproj_kernel(x_ref, g_ref, w_ref, wkT_ref, avg_ref, lng_ref, lnb_ref, sgw_ref, sgb_ref,
                   wbsg_ref, q_ref, kT_ref, v_ref, ga_ref, mb_ref, wm_scr, ysg_scr):
    pair_rows = 2 * SG_BLOCK
    d = x_ref.shape[1]
    bounds = np.cumsum([0, ATT_WIDTH, ATT_WIDTH, ATT_WIDTH, SG_WIDTH, SG_WIDTH, d, d])
    wq_ref, _, wv_ref, wu_ref, wvs_ref, wga_ref, wgb_ref = (
        w_ref.at[:, int(lo):int(hi)] for lo, hi in zip(bounds[:-1], bounds[1:]))

    @pl.when(pl.program_id(0) == 0)
    def _():
        t = lax.broadcasted_iota(jnp.int32, (pair_rows, SG_BLOCK), 0) % SG_BLOCK
        s = lax.broadcasted_iota(jnp.int32, (pair_rows, SG_BLOCK), 1)
        mask = (s // CHUNK) <= (t // CHUNK)
        for j in range(SG_GROUPS // 2):
            wm_scr[j] = jnp.where(mask, sgw_ref[j], 0.0).astype(BF16)

    hb = _rmsnorm(x_ref[...], g_ref[...]).astype(BF16)
    vs_raw = _dot(hb, wvs_ref[...])
    u_raw = _dot(hb, wu_ref[...])
    q_ref[...] = (_dot(hb, wq_ref[...]) * (ATT_HEAD_DIM ** -0.5 * LOG2E)).astype(BF16)
    vs = jax.nn.gelu(vs_raw)

    def group_mean(t):
        tb = t.astype(BF16)
        w = avg_ref.shape[0]
        return jnp.concatenate([_dot(tb[:, c:c + w], avg_ref[...])
                                for c in range(0, SG_WIDTH, w)], axis=1)

    mean = group_mean(vs)
    kT_ref[...] = _dot_nt(wkT_ref[...], hb).astype(BF16)
    dev = vs - mean
    var = group_mean(dev * dev)
    v_ref[...] = _dot(hb, wv_ref[...]).astype(BF16)
    ga_raw = _dot(hb, wga_ref[...])
    vln = (dev * lax.rsqrt(var + EPS) * lng_ref[...] + lnb_ref[...]).astype(BF16)
    u = jax.nn.gelu(u_raw)

    lane = lax.broadcasted_iota(jnp.int32, (SG_BLOCK, LANES), 1)
    for n in range(x_ref.shape[0] // SG_BLOCK):
        rows = slice(n * SG_BLOCK, (n + 1) * SG_BLOCK)
        for j in range(SG_GROUPS // 2):
            cols = slice(j * LANES, (j + 1) * LANES)
            r = _dot(wm_scr[j], vln[rows, cols])
            sv = jnp.where(lane < SG_GROUP_DIM, r[:SG_BLOCK], r[SG_BLOCK:]) + sgb_ref[:, cols]
            ysg_scr[rows, cols] = (u[rows, cols] * sv).astype(BF16)

    gate_b = jax.nn.sigmoid(_dot(hb, wgb_ref[...]))
    mb_ref[...] = (gate_b * _dot(ysg_scr[...], wbsg_ref[...])).astype(BF16)
    ga_ref[...] = jax.nn.sigmoid(ga_raw).astype(BF16)


def _inproj(x, g, w, wkT, avg, lng, lnb, sgw, sgb, wbsg):
    n, d = x.shape
    tm = INPROJ_TILE
    assert tm % SG_BLOCK == 0
    tile = lambda w: pl.BlockSpec((tm, w), lambda i: (i, 0))
    return pl.pallas_call(
        _inproj_kernel,
        grid=(n // tm,),
        out_shape=(jax.ShapeDtypeStruct((n, ATT_WIDTH), BF16),
                   jax.ShapeDtypeStruct((ATT_WIDTH, n), BF16),
                   jax.ShapeDtypeStruct((n, ATT_WIDTH), BF16),
                   jax.ShapeDtypeStruct((n, d), BF16),
                   jax.ShapeDtypeStruct((n, d), BF16)),
        in_specs=[tile(d), _const_spec((1, d)), _const_spec(w.shape), _const_spec(wkT.shape),
                  _const_spec(avg.shape), _const_spec(lng.shape), _const_spec(lnb.shape),
                  _const_spec(sgw.shape), _const_spec(sgb.shape), _const_spec(wbsg.shape)],
        out_specs=(tile(ATT_WIDTH), pl.BlockSpec((ATT_WIDTH, tm), lambda i: (0, i)),
                   tile(ATT_WIDTH), tile(d), tile(d)),
        scratch_shapes=[pltpu.VMEM((SG_GROUPS // 2, 2 * SG_BLOCK, SG_BLOCK), BF16),
                        pltpu.VMEM((tm, SG_WIDTH), BF16)],
        compiler_params=_params(),
        name="inproj_sgu",
    )(x, g, w, wkT, avg, lng, lnb, sgw, sgb, wbsg)


def _attn_kernel(tiles_per_seq, x_ref, q_ref, kTp_ref, kTc_ref, vp_ref, vc_ref, ga_ref, mb_ref,
                 bias_ref, wbatt_ref, wout_ref, gx_ref, wxq_ref, kmT_ref, vm_ref, wxo_ref,
                 o_ref, kT_scr, v_scr, yatt_scr, xo_scr):
    tm = x_ref.shape[0]
    d = x_ref.shape[1]
    first_tile = (pl.program_id(0) % tiles_per_seq) == 0

    kT_scr[:, :tm] = kTp_ref[...]
    kT_scr[:, tm:] = kTc_ref[...]
    vcat = jnp.concatenate([vp_ref[...], vc_ref[...]], axis=0)
    low_half = (lax.broadcasted_iota(jnp.int32, vcat.shape, 1) % LANES) < ATT_HEAD_DIM
    v_scr[0] = jnp.where(low_half, vcat, jnp.ones_like(vcat))
    v_scr[1] = jnp.where(low_half, jnp.ones_like(vcat), vcat)

    lane = lax.broadcasted_iota(jnp.int32, (Q_SUB, LANES), 1)
    n_blocks = WINDOW // LANES

    def window(j, h):
        q0 = j * Q_SUB
        cols = slice((h // 2) * LANES, (h // 2 + 1) * LANES)
        return q0, tm + q0 - PREV_KEYS, cols

    def scores(j, p):
        q0, w0, cols = window(j, 2 * p)
        qp = q_ref[q0:q0 + Q_SUB, cols]
        zero = jnp.zeros_like(qp)
        qz = jnp.concatenate([jnp.where(lane < ATT_HEAD_DIM, qp, zero),
                              jnp.where(lane < ATT_HEAD_DIM, zero, qp)], axis=0)
        s = _dot(qz, kT_scr[cols, w0:w0 + WINDOW])
        heads = []
        for hh in range(2):
            h = 2 * p + hh
            blocks = []
            for b in range(n_blocks):
                before_start = (b + 1) * LANES <= PREV_KEYS - q0
                entry = jnp.where(first_tile, ATT_HEADS, h) if before_start else h
                blk = slice(b * LANES, (b + 1) * LANES)
                blocks.append(s[hh * Q_SUB:(hh + 1) * Q_SUB, blk] + bias_ref[entry, :, blk])
            heads.append(jnp.concatenate(blocks, axis=1))
        return heads

    def attend(j, h, s):
        _, w0, cols = window(j, h)
        p = jnp.exp2(s - jnp.max(s, axis=-1, keepdims=True)).astype(BF16)
        o = _dot(p, v_scr[h % 2, w0:w0 + WINDOW, cols])
        return o / pltpu.roll(o, ATT_HEAD_DIM, axis=1)

    steps = [(j, p) for j in range(tm // Q_SUB) for p in range(ATT_HEADS // 2)]
    pending = [scores(*st) for st in steps[:SCORE_LOOKAHEAD]]
    for k, (j, p) in enumerate(steps):
        if k + SCORE_LOOKAHEAD < len(steps):
            pending.append(scores(*steps[k + SCORE_LOOKAHEAD]))
        s_even, s_odd = pending.pop(0)
        o_even = attend(j, 2 * p, s_even)
        o_odd = attend(j, 2 * p + 1, s_odd)
        q0, _, cols = window(j, 2 * p)
        y = jnp.where(lane < ATT_HEAD_DIM, o_even, o_odd)
        yatt_scr[q0:q0 + Q_SUB, cols] = y.astype(BF16)

    att = _dot(yatt_scr[...], wbatt_ref[...])
    merged = ga_ref[...].astype(F32) * att + mb_ref[...].astype(F32)
    x1 = x_ref[...] + _dot(merged.astype(BF16), wout_ref[...])

    hx = _rmsnorm(x1, gx_ref[...]).astype(BF16)
    xd = d // XATT_HEADS
    qx = (_dot(hx, wxq_ref[...]) * (xd ** -0.5)).astype(BF16)
    for h in range(XATT_HEADS):
        cols = slice(h * xd, (h + 1) * xd)
        s = _dot(qx[:, cols], kmT_ref[0, cols, :])
        e = jnp.exp(s - jnp.max(s, axis=-1, keepdims=True))
        den = jnp.sum(e, axis=-1, keepdims=True)
        xo_scr[:, cols] = (_dot(e.astype(BF16), vm_ref[0, :, cols]) / den).astype(BF16)
    o_ref[...] = x1 + _dot(xo_scr[...], wxo_ref[...])


def _attn(x, q, kT, v, ga, mb, bias, wbatt, wout, gx, wxq, kmT, vm, wxo, seq):
    n, d = x.shape
    tm = TOKEN_TILE
    tps = seq // tm
    mem_len = vm.shape[1]
    tile = lambda w: pl.BlockSpec((tm, w), lambda i: (i, 0))
    prev = lambda i: jnp.maximum(i - 1, 0)
    return pl.pallas_call(
        functools.partial(_attn_kernel, tps),
        grid=(n // tm,),
        out_shape=jax.ShapeDtypeStruct((n, d), F32),
        in_specs=[tile(d), tile(ATT_WIDTH),
                  pl.BlockSpec((ATT_WIDTH, tm), lambda i: (0, prev(i))),
                  pl.BlockSpec((ATT_WIDTH, tm), lambda i: (0, i)),
                  pl.BlockSpec((tm, ATT_WIDTH), lambda i: (prev(i), 0)),
                  tile(ATT_WIDTH), tile(d), tile(d),
                  _const_spec(bias.shape), _const_spec(wbatt.shape), _const_spec(wout.shape),
                  _const_spec((1, d)), _const_spec(wxq.shape),
                  pl.BlockSpec((1, d, mem_len), lambda i: (i // tps, 0, 0)),
                  pl.BlockSpec((1, mem_len, d), lambda i: (i // tps, 0, 0)),
                  _const_spec(wxo.shape)],
        out_specs=tile(d),
        scratch_shapes=[pltpu.VMEM((ATT_WIDTH, 2 * tm), BF16),
                        pltpu.VMEM((2, 2 * tm, ATT_WIDTH), BF16),
                        pltpu.VMEM((tm, ATT_WIDTH), BF16),
                        pltpu.VMEM((tm, d), BF16)],
        compiler_params=_params(),
        name="attn_merge_xattn",
    )(x, q, kT, kT, v, v, ga, mb, bias, wbatt, wout, gx, wxq, kmT, vm, wxo)


def _ffn_kernel(x_ref, g_ref, wfi_ref, wfo_ref, gfin_ref, o_ref, a_scr):
    d_ff = wfo_ref.shape[0]
    subs = [slice(r, r + FFN_SUB) for r in range(0, x_ref.shape[0], FFN_SUB)]
    hb_next = _rmsnorm(x_ref[subs[0], :], g_ref[...]).astype(BF16)
    for r, rows in enumerate(subs):
        hb = hb_next
        for c in range(d_ff // FF_CHUNK):
            gate = _dot(hb, wfi_ref[:, c * FF_CHUNK:(c + 1) * FF_CHUNK])
            up = _dot(hb, wfi_ref[:, d_ff + c * FF_CHUNK:d_ff + (c + 1) * FF_CHUNK])
            a_scr[rows, c * FF_CHUNK:(c + 1) * FF_CHUNK] = (jax.nn.silu(gate) * up).astype(BF16)
            if c == 0 and r + 1 < len(subs):
                hb_next = _rmsnorm(x_ref[subs[r + 1], :], g_ref[...]).astype(BF16)
        x3 = x_ref[rows, :] + _dot(a_scr[rows, :], wfo_ref[...])
        o_ref[rows, :] = _rmsnorm(x3, gfin_ref[...])


def _ffn(x, g, wfi, wfo, gfin):
    n, d = x.shape
    tm = FFN_TILE
    d_ff = wfo.shape[0]
    assert d_ff % FF_CHUNK == 0
    tile = pl.BlockSpec((tm, d), lambda i: (i, 0))
    return pl.pallas_call(
        _ffn_kernel,
        grid=(n // tm,),
        out_shape=jax.ShapeDtypeStruct((n, d), F32),
        in_specs=[tile, _const_spec((1, d)), _const_spec(wfi.shape), _const_spec(wfo.shape),
                  _const_spec((1, d))],
        out_specs=tile,
        scratch_shapes=[pltpu.VMEM((tm, d_ff), BF16)],
        compiler_params=_params(),
        name="ffn_final_norm",
    )(x, g, wfi, wfo, gfin)


def kernel(x, mem, norm_mix_g, w_in, rel_bias, sg_ln_g, sg_ln_b, sg_w, sg_b, w_branch_att,
           w_branch_sg, w_out, norm_xattn_g, norm_mem_g, w_xq, w_xkv, w_xo, norm_ffn_g,
           w_ffn_in, w_ffn_out, norm_final_g):
    b, s, d = x.shape
    depth = w_in.shape[0]
    assert s % TOKEN_TILE == 0 and TOKEN_TILE >= PREV_KEYS and TOKEN_TILE % Q_SUB == 0
    assert (b * s) % FFN_TILE == 0 and (b * s) % INPROJ_TILE == 0
    row = lambda a: a.reshape(1, -1).astype(F32)
    grp = np.arange(MXU_WIDTH) // SG_GROUP_DIM
    avg = jnp.asarray((grp[:, None] == grp[None, :]) / SG_GROUP_DIM, BF16)

    xf = x.reshape(b * s, d)
    for l in range(depth):
        wl = w_in[l].astype(BF16)
        wkT = wl[:, ATT_WIDTH:2 * ATT_WIDTH].T
        sgw = sg_w[l].reshape(SG_GROUPS // 2, 2 * SG_BLOCK, SG_BLOCK)
        sgb = jnp.repeat(sg_b[l].T, SG_GROUP_DIM, axis=1)
        wxkv = w_xkv[l].astype(BF16)

        bias = _bias_table(rel_bias[l])
        kmT, vm = _mem_kv(mem, row(norm_mem_g[l]), wxkv[:, :d].T, wxkv)
        q, kT, v, ga, mb = _inproj(
            xf, row(norm_mix_g[l]), wl, wkT, avg,
            row(sg_ln_g[l]), row(sg_ln_b[l]), sgw, sgb, w_branch_sg[l].astype(BF16))
        xf = _attn(xf, q, kT, v, ga, mb, bias, w_branch_att[l].astype(BF16),
                   w_out[l].astype(BF16), row(norm_xattn_g[l]), w_xq[l].astype(BF16),
                   kmT, vm, w_xo[l].astype(BF16), s)
        assert depth == 1
        xf = _ffn(xf, row(norm_ffn_g[l]), w_ffn_in[l].astype(BF16), w_ffn_out[l].astype(BF16),
                  row(norm_final_g))
    return xf.reshape(b, s, d)
```

```python
import functools

import jax
import jax.numpy as jnp
import numpy as np
from jax import lax
from jax.experimental import pallas as pl
from jax.experimental.pallas import tpu as pltpu

CHUNK = 64
N_PREV_CHUNKS = 8
ATT_HEADS = 8
ATT_HEAD_DIM = 64
ATT_WIDTH = ATT_HEADS * ATT_HEAD_DIM
REL_CLIP = 128
SG_BLOCK = 128
SG_GROUPS = 8
SG_GROUP_DIM = 64
SG_WIDTH = SG_GROUPS * SG_GROUP_DIM
XATT_HEADS = 4
EPS = 1e-6
NEG_INF = -1e30
LOG2E = 1.4426950408889634

LANES = 128
MXU_WIDTH = 256
VMEM_LIMIT_BYTES = 56 * 1024 * 1024

TOKEN_TILE = 512
INPROJ_TILE = 1024
FFN_TILE = 1024
FFN_SUB = 512
Q_SUB = 2 * CHUNK
PREV_KEYS = N_PREV_CHUNKS * CHUNK
WINDOW = PREV_KEYS + Q_SUB
FF_CHUNK = 256
SCORE_LOOKAHEAD = 2

F32 = jnp.float32
BF16 = jnp.bfloat16
_NT = (((1,), (1,)), ((), ()))


def _dot(a, b):
    return jnp.dot(a, b, preferred_element_type=F32)


def _dot_nt(a, b):
    return lax.dot_general(a, b, _NT, preferred_element_type=F32)


def _rmsnorm(x, g):
    return x * lax.rsqrt(jnp.mean(x * x, axis=-1, keepdims=True) + EPS) * g


def _const_spec(shape):
    zeros = (0,) * len(shape)
    return pl.BlockSpec(shape, lambda *_: zeros, pipeline_mode=pl.Buffered(1))


def _params():
    return pltpu.CompilerParams(dimension_semantics=("arbitrary",),
                                vmem_limit_bytes=VMEM_LIMIT_BYTES)


def _bias_kernel(rb_ref, o_ref):
    near_cols = Q_SUB + REL_CLIP
    far_cols = WINDOW - near_cols
    width = near_cols + Q_SUB
    c = lax.broadcasted_iota(jnp.int32, (1, width), 1)
    idx = jnp.clip(near_cols - c, -REL_CLIP, REL_CLIP) + REL_CLIP
    diag = jnp.zeros((ATT_HEADS, width), F32)
    for j in range(2 * REL_CLIP + 1):
        diag = jnp.where(idx == j, rb_ref[:, j:j + 1], diag)
    diag = diag * LOG2E

    row = lax.broadcasted_iota(jnp.int32, (Q_SUB, width), 0)
    qc = lax.broadcasted_iota(jnp.int32, (Q_SUB, WINDOW), 0) // CHUNK
    kc = lax.broadcasted_iota(jnp.int32, (Q_SUB, WINDOW), 1) // CHUNK
    valid = (kc >= qc) & (kc <= qc + N_PREV_CHUNKS)
    for h in range(ATT_HEADS):
        t = jnp.broadcast_to(diag[h:h + 1, :], (Q_SUB, width))
        for bit in range(Q_SUB.bit_length() - 1):
            t = jnp.where(((row >> bit) & 1) == 1, pltpu.roll(t, 1 << bit, axis=1), t)
        far = jnp.broadcast_to(diag[h:h + 1, 0:1], (Q_SUB, far_cols))
        o_ref[h] = jnp.where(valid, jnp.concatenate([far, t[:, Q_SUB:]], axis=1), NEG_INF)
    o_ref[ATT_HEADS] = jnp.full((Q_SUB, WINDOW), NEG_INF, F32)


def _bias_table(rel_bias):
    return pl.pallas_call(
        _bias_kernel,
        out_shape=jax.ShapeDtypeStruct((ATT_HEADS + 1, Q_SUB, WINDOW), F32),
        in_specs=[pl.BlockSpec(memory_space=pltpu.VMEM)],
        out_specs=pl.BlockSpec(memory_space=pltpu.VMEM),
        name="bias_table",
    )(rel_bias)


def _memkv_kernel(mem_ref, g_ref, wkT_ref, wv_ref, kT_ref, v_ref):
    mn = _rmsnorm(mem_ref[0], g_ref[...]).astype(BF16)
    kT_ref[0] = _dot_nt(wkT_ref[...], mn).astype(BF16)
    v_ref[0] = _dot(mn, wv_ref[...]).astype(BF16)


def _mem_kv(mem, g, wkT, wv):
    b, m, d = mem.shape
    return pl.pallas_call(
        _memkv_kernel,
        grid=(b,),
        out_shape=(jax.ShapeDtypeStruct((b, d, m), BF16), jax.ShapeDtypeStruct((b, m, d), BF16)),
        in_specs=[pl.BlockSpec((1, m, d), lambda i: (i, 0, 0)),
                  _const_spec((1, d)), _const_spec((d, d)), _const_spec((d, d))],
        out_specs=(pl.BlockSpec((1, d, m), lambda i: (i, 0, 0)),
                   pl.BlockSpec((1, m, d), lambda i: (i, 0, 0))),
        compiler_params=_params(),
        name="mem_kv",
    )(mem, g, wkT, wv)


def _inproj_kernel(x_ref, g_ref, wq_ref, wk_ref, wv_ref, wu_ref, wvs_ref, wga_ref, wgb_ref,
                   avg_ref, lng_ref, lnb_ref, sgw_ref, sgb_ref, wbsg_ref,
                   q_ref, kT_ref, v_ref, ga_ref, mb_ref, wm_scr, ysg_scr):
    pair_rows = 2 * SG_BLOCK

    @pl.when(pl.program_id(0) == 0)
    def _():
        t = lax.broadcasted_iota(jnp.int32, (pair_rows, SG_BLOCK), 0) % SG_BLOCK
        s = lax.broadcasted_iota(jnp.int32, (pair_rows, SG_BLOCK), 1)
        mask = (s // CHUNK) <= (t // CHUNK)
        for j in range(SG_GROUPS // 2):
            wm_scr[j] = jnp.where(mask, sgw_ref[j], 0.0).astype(BF16)

    hb = _rmsnorm(x_ref[...], g_ref[...]).astype(BF16)
    vs_raw = _dot(hb, wvs_ref[...])
    u_raw = _dot(hb, wu_ref[...])
    q_ref[...] = (_dot(hb, wq_ref[...]) * (ATT_HEAD_DIM ** -0.5 * LOG2E)).astype(BF16)
    vs = jax.nn.gelu(vs_raw)

    def group_mean(t):
        tb = t.astype(BF16)
        w = avg_ref.shape[0]
        return jnp.concatenate([_dot(tb[:, c:c + w], avg_ref[...])
                                for c in range(0, SG_WIDTH, w)], axis=1)

    mean = group_mean(vs)
    kT_ref[...] = _dot(hb, wk_ref[...]).T.astype(BF16)
    dev = vs - mean
    var = group_mean(dev * dev)
    v_ref[...] = _dot(hb, wv_ref[...]).astype(BF16)
    ga_raw = _dot(hb, wga_ref[...])
    vln = (dev * lax.rsqrt(var + EPS) * lng_ref[...] + lnb_ref[...]).astype(BF16)
    u = jax.nn.gelu(u_raw)

    lane = lax.broadcasted_iota(jnp.int32, (SG_BLOCK, LANES), 1)
    for n in range(x_ref.shape[0] // SG_BLOCK):
        rows = slice(n * SG_BLOCK, (n + 1) * SG_BLOCK)
        for j in range(SG_GROUPS // 2):
            cols = slice(j * LANES, (j + 1) * LANES)
            r = _dot(wm_scr[j], vln[rows, cols])
            sv = jnp.where(lane < SG_GROUP_DIM, r[:SG_BLOCK], r[SG_BLOCK:]) + sgb_ref[:, cols]
            ysg_scr[rows, cols] = (u[rows, cols] * sv).astype(BF16)

    gate_b = jax.nn.sigmoid(_dot(hb, wgb_ref[...]))
    mb_ref[...] = (gate_b * _dot(ysg_scr[...], wbsg_ref[...])).astype(BF16)
    ga_ref[...] = jax.nn.sigmoid(ga_raw).astype(BF16)


def _inproj(x, g, wq, wk, wv, wu, wvs, wga, wgb, avg, lng, lnb, sgw, sgb, wbsg):
    n, d = x.shape
    tm = INPROJ_TILE
    assert tm % SG_BLOCK == 0
    tile = lambda w: pl.BlockSpec((tm, w), lambda i: (i, 0))
    return pl.pallas_call(
        _inproj_kernel,
        grid=(n // tm,),
        out_shape=(jax.ShapeDtypeStruct((n, ATT_WIDTH), BF16),
                   jax.ShapeDtypeStruct((ATT_WIDTH, n), BF16),
                   jax.ShapeDtypeStruct((n, ATT_WIDTH), BF16),
                   jax.ShapeDtypeStruct((n, d), BF16),
                   jax.ShapeDtypeStruct((n, d), BF16)),
        in_specs=[tile(d), _const_spec((1, d)),
                  _const_spec(wq.shape), _const_spec(wk.shape), _const_spec(wv.shape),
                  _const_spec(wu.shape), _const_spec(wvs.shape), _const_spec(wga.shape),
                  _const_spec(wgb.shape), _const_spec(avg.shape), _const_spec(lng.shape),
                  _const_spec(lnb.shape), _const_spec(sgw.shape), _const_spec(sgb.shape),
                  _const_spec(wbsg.shape)],
        out_specs=(tile(ATT_WIDTH), pl.BlockSpec((ATT_WIDTH, tm), lambda i: (0, i)),
                   tile(ATT_WIDTH), tile(d), tile(d)),
        scratch_shapes=[pltpu.VMEM((SG_GROUPS // 2, 2 * SG_BLOCK, SG_BLOCK), BF16),
                        pltpu.VMEM((tm, SG_WIDTH), BF16)],
        compiler_params=_params(),
        name="inproj_sgu",
    )(x, g, wq, wk, wv, wu, wvs, wga, wgb, avg, lng, lnb, sgw, sgb, wbsg)


def _attn_kernel(tiles_per_seq, x_ref, q_ref, kTp_ref, kTc_ref, vp_ref, vc_ref, ga_ref, mb_ref,
                 bias_ref, wbatt_ref, wout_ref, gx_ref, wxq_ref, kmT_ref, vm_ref, wxo_ref,
                 o_ref, kT_scr, v_scr, yatt_scr, xo_scr):
    tm = x_ref.shape[0]
    d = x_ref.shape[1]
    first_tile = (pl.program_id(0) % tiles_per_seq) == 0

    kT_scr[:, :tm] = kTp_ref[...]
    kT_scr[:, tm:] = kTc_ref[...]
    vcat = jnp.concatenate([vp_ref[...], vc_ref[...]], axis=0)
    low_half = (lax.broadcasted_iota(jnp.int32, vcat.shape, 1) % LANES) < ATT_HEAD_DIM
    v_scr[0] = jnp.where(low_half, vcat, jnp.ones_like(vcat))
    v_scr[1] = jnp.where(low_half, jnp.ones_like(vcat), vcat)

    lane = lax.broadcasted_iota(jnp.int32, (Q_SUB, LANES), 1)
    n_blocks = WINDOW // LANES

    def window(j, h):
        q0 = j * Q_SUB
        cols = slice((h // 2) * LANES, (h // 2 + 1) * LANES)
        return q0, tm + q0 - PREV_KEYS, cols

    def scores(j, p):
        q0, w0, cols = window(j, 2 * p)
        qp = q_ref[q0:q0 + Q_SUB, cols]
        zero = jnp.zeros_like(qp)
        qz = jnp.concatenate([jnp.where(lane < ATT_HEAD_DIM, qp, zero),
                              jnp.where(lane < ATT_HEAD_DIM, zero, qp)], axis=0)
        s = _dot(qz, kT_scr[cols, w0:w0 + WINDOW])
        heads = []
        for hh in range(2):
            h = 2 * p + hh
            blocks = []
            for b in range(n_blocks):
                before_start = (b + 1) * LANES <= PREV_KEYS - q0
                entry = jnp.where(first_tile, ATT_HEADS, h) if before_start else h
                blk = slice(b * LANES, (b + 1) * LANES)
                blocks.append(s[hh * Q_SUB:(hh + 1) * Q_SUB, blk] + bias_ref[entry, :, blk])
            heads.append(jnp.concatenate(blocks, axis=1))
        return heads

    def attend(j, h, s):
        _, w0, cols = window(j, h)
        p = jnp.exp2(s - jnp.max(s, axis=-1, keepdims=True)).astype(BF16)
        o = _dot(p, v_scr[h % 2, w0:w0 + WINDOW, cols])
        return o / pltpu.roll(o, ATT_HEAD_DIM, axis=1)

    steps = [(j, p) for j in range(tm // Q_SUB) for p in range(ATT_HEADS // 2)]
    pending = [scores(*st) for st in steps[:SCORE_LOOKAHEAD]]
    for k, (j, p) in enumerate(steps):
        if k + SCORE_LOOKAHEAD < len(steps):
            pending.append(scores(*steps[k + SCORE_LOOKAHEAD]))
        s_even, s_odd = pending.pop(0)
        o_even = attend(j, 2 * p, s_even)
        o_odd = attend(j, 2 * p + 1, s_odd)
        q0, _, cols = window(j, 2 * p)
        y = jnp.where(lane < ATT_HEAD_DIM, o_even, o_odd)
        yatt_scr[q0:q0 + Q_SUB, cols] = y.astype(BF16)

    att = _dot(yatt_scr[...], wbatt_ref[...])
    merged = ga_ref[...].astype(F32) * att + mb_ref[...].astype(F32)
    x1 = x_ref[...] + _dot(merged.astype(BF16), wout_ref[...])

    hx = _rmsnorm(x1, gx_ref[...]).astype(BF16)
    xd = d // XATT_HEADS
    qx = (_dot(hx, wxq_ref[...]) * (xd ** -0.5)).astype(BF16)
    for h in range(XATT_HEADS):
        cols = slice(h * xd, (h + 1) * xd)
        s = _dot(qx[:, cols], kmT_ref[0, cols, :])
        e = jnp.exp(s - jnp.max(s, axis=-1, keepdims=True))
        den = jnp.sum(e, axis=-1, keepdims=True)
        xo_scr[:, cols] = (_dot(e.astype(BF16), vm_ref[0, :, cols]) / den).astype(BF16)
    o_ref[...] = x1 + _dot(xo_scr[...], wxo_ref[...])


def _attn(x, q, kT, v, ga, mb, bias, wbatt, wout, gx, wxq, kmT, vm, wxo, seq):
    n, d = x.shape
    tm = TOKEN_TILE
    tps = seq // tm
    mem_len = vm.shape[1]
    tile = lambda w: pl.BlockSpec((tm, w), lambda i: (i, 0))
    prev = lambda i: jnp.maximum(i - 1, 0)
    return pl.pallas_call(
        functools.partial(_attn_kernel, tps),
        grid=(n // tm,),
        out_shape=jax.ShapeDtypeStruct((n, d), F32),
        in_specs=[tile(d), tile(ATT_WIDTH),
                  pl.BlockSpec((ATT_WIDTH, tm), lambda i: (0, prev(i))),
                  pl.BlockSpec((ATT_WIDTH, tm), lambda i: (0, i)),
                  pl.BlockSpec((tm, ATT_WIDTH), lambda i: (prev(i), 0)),
                  tile(ATT_WIDTH), tile(d), tile(d),
                  _const_spec(bias.shape), _const_spec(wbatt.shape), _const_spec(wout.shape),
                  _const_spec((1, d)), _const_spec(wxq.shape),
                  pl.BlockSpec((1, d, mem_len), lambda i: (i // tps, 0, 0)),
                  pl.BlockSpec((1, mem_len, d), lambda i: (i // tps, 0, 0)),
                  _const_spec(wxo.shape)],
        out_specs=tile(d),
        scratch_shapes=[pltpu.VMEM((ATT_WIDTH, 2 * tm), BF16),
                        pltpu.VMEM((2, 2 * tm, ATT_WIDTH), BF16),
                        pltpu.VMEM((tm, ATT_WIDTH), BF16),
                        pltpu.VMEM((tm, d), BF16)],
        compiler_params=_params(),
        name="attn_merge_xattn",
    )(x, q, kT, kT, v, v, ga, mb, bias, wbatt, wout, gx, wxq, kmT, vm, wxo)


def _ffn_kernel(x_ref, g_ref, wfi_ref, wfo_ref, gfin_ref, o_ref, a_scr):
    d_ff = wfo_ref.shape[0]
    subs = [slice(r, r + FFN_SUB) for r in range(0, x_ref.shape[0], FFN_SUB)]
    hb_next = _rmsnorm(x_ref[subs[0], :], g_ref[...]).astype(BF16)
    for r, rows in enumerate(subs):
        hb = hb_next
        for c in range(d_ff // FF_CHUNK):
            gate = _dot(hb, wfi_ref[:, c * FF_CHUNK:(c + 1) * FF_CHUNK])
            up = _dot(hb, wfi_ref[:, d_ff + c * FF_CHUNK:d_ff + (c + 1) * FF_CHUNK])
            a_scr[rows, c * FF_CHUNK:(c + 1) * FF_CHUNK] = (jax.nn.silu(gate) * up).astype(BF16)
            if c == 0 and r + 1 < len(subs):
                hb_next = _rmsnorm(x_ref[subs[r + 1], :], g_ref[...]).astype(BF16)
        x3 = x_ref[rows, :] + _dot(a_scr[rows, :], wfo_ref[...])
        o_ref[rows, :] = _rmsnorm(x3, gfin_ref[...])


def _ffn(x, g, wfi, wfo, gfin):
    n, d = x.shape
    tm = FFN_TILE
    d_ff = wfo.shape[0]
    assert d_ff % FF_CHUNK == 0
    tile = pl.BlockSpec((tm, d), lambda i: (i, 0))
    return pl.pallas_call(
        _ffn_kernel,
        grid=(n // tm,),
        out_shape=jax.ShapeDtypeStruct((n, d), F32),
        in_specs=[tile, _const_spec((1, d)), _const_spec(wfi.shape), _const_spec(wfo.shape),
                  _const_spec((1, d))],
        out_specs=tile,
        scratch_shapes=[pltpu.VMEM((tm, d_ff), BF16)],
        compiler_params=_params(),
        name="ffn_final_norm",
    )(x, g, wfi, wfo, gfin)


def kernel(x, mem, norm_mix_g, w_in, rel_bias, sg_ln_g, sg_ln_b, sg_w, sg_b, w_branch_att,
           w_branch_sg, w_out, norm_xattn_g, norm_mem_g, w_xq, w_xkv, w_xo, norm_ffn_g,
           w_ffn_in, w_ffn_out, norm_final_g):
    b, s, d = x.shape
    depth = w_in.shape[0]
    assert s % TOKEN_TILE == 0 and TOKEN_TILE >= PREV_KEYS and TOKEN_TILE % Q_SUB == 0
    assert (b * s) % FFN_TILE == 0 and (b * s) % INPROJ_TILE == 0
    row = lambda a: a.reshape(1, -1).astype(F32)
    grp = np.arange(MXU_WIDTH) // SG_GROUP_DIM
    avg = jnp.asarray((grp[:, None] == grp[None, :]) / SG_GROUP_DIM, BF16)

    xf = x.reshape(b * s, d)
    for l in range(depth):
        bounds = np.cumsum([0, ATT_WIDTH, ATT_WIDTH, ATT_WIDTH, SG_WIDTH, SG_WIDTH, d, d])
        wq, wk, wv, wu, wvs, wga, wgb = (
            w_in[l, :, int(lo):int(hi)].astype(BF16) for lo, hi in zip(bounds[:-1], bounds[1:]))
        sgw = sg_w[l].reshape(SG_GROUPS // 2, 2 * SG_BLOCK, SG_BLOCK)
        sgb = jnp.repeat(sg_b[l].T, SG_GROUP_DIM, axis=1)

        bias = _bias_table(rel_bias[l])
        kmT, vm = _mem_kv(mem, row(norm_mem_g[l]), w_xkv[l, :, :d].T.astype(BF16),
                          w_xkv[l, :, d:].astype(BF16))
        q, kT, v, ga, mb = _inproj(
            xf, row(norm_mix_g[l]), wq, wk, wv, wu, wvs, wga, wgb, avg,
            row(sg_ln_g[l]), row(sg_ln_b[l]), sgw, sgb, w_branch_sg[l].astype(BF16))
        xf = _attn(xf, q, kT, v, ga, mb, bias, w_branch_att[l].astype(BF16),
                   w_out[l].astype(BF16), row(norm_xattn_g[l]), w_xq[l].astype(BF16),
                   kmT, vm, w_xo[l].astype(BF16), s)
        assert depth == 1
        xf = _ffn(xf, row(norm_ffn_g[l]), w_ffn_in[l].astype(BF16), w_ffn_out[l].astype(BF16),
                  row(norm_final_g))
    return xf.reshape(b, s, d)
```

```python
import functools

import jax
import jax.numpy as jnp
import numpy as np
from jax import lax
from jax.experimental import pallas as pl
from jax.experimental.pallas import tpu as pltpu

CHUNK = 64
N_PREV_CHUNKS = 8
ATT_HEADS = 8
ATT_HEAD_DIM = 64
ATT_WIDTH = ATT_HEADS * ATT_HEAD_DIM
REL_CLIP = 128
SG_BLOCK = 128
SG_GROUPS = 8
SG_GROUP_DIM = 64
SG_WIDTH = SG_GROUPS * SG_GROUP_DIM
XATT_HEADS = 4
EPS = 1e-6
NEG_INF = -1e30
LOG2E = 1.4426950408889634

LANES = 128
MXU_WIDTH = 256
VMEM_LIMIT_BYTES = 62 * 1024 * 1024

TOKEN_TILE = 1024
INPROJ_TILE = 1024
FFN_TILE = 1024
FFN_SUB = 512
Q_SUB = 2 * CHUNK
PREV_KEYS = N_PREV_CHUNKS * CHUNK
WINDOW = PREV_KEYS + Q_SUB
FF_CHUNK = 256
SCORE_LOOKAHEAD = 2

F32 = jnp.float32
BF16 = jnp.bfloat16
_NT = (((1,), (1,)), ((), ()))


def _dot(a, b):
    return jnp.dot(a, b, preferred_element_type=F32)


def _dot_nt(a, b):
    return lax.dot_general(a, b, _NT, preferred_element_type=F32)


def _rmsnorm(x, g):
    return x * lax.rsqrt(jnp.mean(x * x, axis=-1, keepdims=True) + EPS) * g


def _const_spec(shape):
    zeros = (0,) * len(shape)
    return pl.BlockSpec(shape, lambda *_: zeros, pipeline_mode=pl.Buffered(1))


def _params():
    return pltpu.CompilerParams(dimension_semantics=("arbitrary",),
                                vmem_limit_bytes=VMEM_LIMIT_BYTES)


def _bias_kernel(rb_ref, o_ref):
    near_cols = Q_SUB + REL_CLIP
    far_cols = WINDOW - near_cols
    width = near_cols + Q_SUB
    c = lax.broadcasted_iota(jnp.int32, (1, width), 1)
    idx = jnp.clip(near_cols - c, -REL_CLIP, REL_CLIP) + REL_CLIP
    diag = jnp.zeros((ATT_HEADS, width), F32)
    for j in range(2 * REL_CLIP + 1):
        diag = jnp.where(idx == j, rb_ref[:, j:j + 1], diag)
    diag = diag * LOG2E

    row = lax.broadcasted_iota(jnp.int32, (Q_SUB, width), 0)
    qc = lax.broadcasted_iota(jnp.int32, (Q_SUB, WINDOW), 0) // CHUNK
    kc = lax.broadcasted_iota(jnp.int32, (Q_SUB, WINDOW), 1) // CHUNK
    valid = (kc >= qc) & (kc <= qc + N_PREV_CHUNKS)
    for h in range(ATT_HEADS):
        t = jnp.broadcast_to(diag[h:h + 1, :], (Q_SUB, width))
        for bit in range(Q_SUB.bit_length() - 1):
            t = jnp.where(((row >> bit) & 1) == 1, pltpu.roll(t, 1 << bit, axis=1), t)
        far = jnp.broadcast_to(diag[h:h + 1, 0:1], (Q_SUB, far_cols))
        o_ref[h] = jnp.where(valid, jnp.concatenate([far, t[:, Q_SUB:]], axis=1), NEG_INF)
    o_ref[ATT_HEADS] = jnp.full((Q_SUB, WINDOW), NEG_INF, F32)


def _bias_table(rel_bias):
    return pl.pallas_call(
        _bias_kernel,
        out_shape=jax.ShapeDtypeStruct((ATT_HEADS + 1, Q_SUB, WINDOW), F32),
        in_specs=[pl.BlockSpec(memory_space=pltpu.VMEM)],
        out_specs=pl.BlockSpec(memory_space=pltpu.VMEM),
        name="bias_table",
    )(rel_bias)


def _memkv_kernel(mem_ref, g_ref, wkT_ref, wv_ref, kT_ref, v_ref):
    mn = _rmsnorm(mem_ref[0], g_ref[...]).astype(BF16)
    kT_ref[0] = _dot_nt(wkT_ref[...], mn).astype(BF16)
    v_ref[0] = _dot(mn, wv_ref[...]).astype(BF16)


def _mem_kv(mem, g, wkT, wv):
    b, m, d = mem.shape
    return pl.pallas_call(
        _memkv_kernel,
        grid=(b,),
        out_shape=(jax.ShapeDtypeStruct((b, d, m), BF16), jax.ShapeDtypeStruct((b, m, d), BF16)),
        in_specs=[pl.BlockSpec((1, m, d), lambda i: (i, 0, 0)),
                  _const_spec((1, d)), _const_spec((d, d)), _const_spec((d, d))],
        out_specs=(pl.BlockSpec((1, d, m), lambda i: (i, 0, 0)),
                   pl.BlockSpec((1, m, d), lambda i: (i, 0, 0))),
        compiler_params=_params(),
        name="mem_kv",
    )(mem, g, wkT, wv)


def _inproj_kernel(x_ref, g_ref, wq_ref, wk_ref, wv_ref, wu_ref, wvs_ref, wga_ref, wgb_ref,
                   avg_ref, lng_ref, lnb_ref, sgw_ref, sgb_ref, wbsg_ref,
                   q_ref, kT_ref, v_ref, ga_ref, mb_ref, wm_scr, ysg_scr):
    pair_rows = 2 * SG_BLOCK

    @pl.when(pl.program_id(0) == 0)
    def _():
        t = lax.broadcasted_iota(jnp.int32, (pair_rows, SG_BLOCK), 0) % SG_BLOCK
        s = lax.broadcasted_iota(jnp.int32, (pair_rows, SG_BLOCK), 1)
        mask = (s // CHUNK) <= (t // CHUNK)
        for j in range(SG_GROUPS // 2):
            wm_scr[j] = jnp.where(mask, sgw_ref[j], 0.0).astype(BF16)

    hb = _rmsnorm(x_ref[...], g_ref[...]).astype(BF16)
    vs_raw = _dot(hb, wvs_ref[...])
    u_raw = _dot(hb, wu_ref[...])
    q_ref[...] = (_dot(hb, wq_ref[...]) * (ATT_HEAD_DIM ** -0.5 * LOG2E)).astype(BF16)
    vs = jax.nn.gelu(vs_raw)

    def group_mean(t):
        tb = t.astype(BF16)
        w = avg_ref.shape[0]
        return jnp.concatenate([_dot(tb[:, c:c + w], avg_ref[...])
                                for c in range(0, SG_WIDTH, w)], axis=1)

    mean = group_mean(vs)
    kT_ref[...] = _dot(hb, wk_ref[...]).T.astype(BF16)
    dev = vs - mean
    var = group_mean(dev * dev)
    v_ref[...] = _dot(hb, wv_ref[...]).astype(BF16)
    ga_raw = _dot(hb, wga_ref[...])
    vln = (dev * lax.rsqrt(var + EPS) * lng_ref[...] + lnb_ref[...]).astype(BF16)
    u = jax.nn.gelu(u_raw)

    lane = lax.broadcasted_iota(jnp.int32, (SG_BLOCK, LANES), 1)
    for n in range(x_ref.shape[0] // SG_BLOCK):
        rows = slice(n * SG_BLOCK, (n + 1) * SG_BLOCK)
        for j in range(SG_GROUPS // 2):
            cols = slice(j * LANES, (j + 1) * LANES)
            r = _dot(wm_scr[j], vln[rows, cols])
            sv = jnp.where(lane < SG_GROUP_DIM, r[:SG_BLOCK], r[SG_BLOCK:]) + sgb_ref[:, cols]
            ysg_scr[rows, cols] = (u[rows, cols] * sv).astype(BF16)

    gate_b = jax.nn.sigmoid(_dot(hb, wgb_ref[...]))
    mb_ref[...] = (gate_b * _dot(ysg_scr[...], wbsg_ref[...])).astype(BF16)
    ga_ref[...] = jax.nn.sigmoid(ga_raw).astype(BF16)


def _inproj(x, g, wq, wk, wv, wu, wvs, wga, wgb, avg, lng, lnb, sgw, sgb, wbsg):
    n, d = x.shape
    tm = INPROJ_TILE
    assert tm % SG_BLOCK == 0
    tile = lambda w: pl.BlockSpec((tm, w), lambda i: (i, 0))
    return pl.pallas_call(
        _inproj_kernel,
        grid=(n // tm,),
        out_shape=(jax.ShapeDtypeStruct((n, ATT_WIDTH), BF16),
                   jax.ShapeDtypeStruct((ATT_WIDTH, n), BF16),
                   jax.ShapeDtypeStruct((n, ATT_WIDTH), BF16),
                   jax.ShapeDtypeStruct((n, d), BF16),
                   jax.ShapeDtypeStruct((n, d), BF16)),
        in_specs=[tile(d), _const_spec((1, d)),
                  _const_spec(wq.shape), _const_spec(wk.shape), _const_spec(wv.shape),
                  _const_spec(wu.shape), _const_spec(wvs.shape), _const_spec(wga.shape),
                  _const_spec(wgb.shape), _const_spec(avg.shape), _const_spec(lng.shape),
                  _const_spec(lnb.shape), _const_spec(sgw.shape), _const_spec(sgb.shape),
                  _const_spec(wbsg.shape)],
        out_specs=(tile(ATT_WIDTH), pl.BlockSpec((ATT_WIDTH, tm), lambda i: (0, i)),
                   tile(ATT_WIDTH), tile(d), tile(d)),
        scratch_shapes=[pltpu.VMEM((SG_GROUPS // 2, 2 * SG_BLOCK, SG_BLOCK), BF16),
                        pltpu.VMEM((tm, SG_WIDTH), BF16)],
        compiler_params=_params(),
        name="inproj_sgu",
    )(x, g, wq, wk, wv, wu, wvs, wga, wgb, avg, lng, lnb, sgw, sgb, wbsg)


def _attn_kernel(tiles_per_seq, x_ref, q_ref, kTp_ref, kTc_ref, vp_ref, vc_ref, ga_ref, mb_ref,
                 bias_ref, wbatt_ref, wout_ref, gx_ref, wxq_ref, kmT_ref, vm_ref, wxo_ref,
                 o_ref, kT_scr, v_scr, yatt_scr, xo_scr):
    tm = x_ref.shape[0]
    d = x_ref.shape[1]
    first_tile = (pl.program_id(0) % tiles_per_seq) == 0

    kT_scr[:, :PREV_KEYS] = kTp_ref[...]
    kT_scr[:, PREV_KEYS:] = kTc_ref[...]
    vcat = jnp.concatenate([vp_ref[...], vc_ref[...]], axis=0)
    low_half = (lax.broadcasted_iota(jnp.int32, vcat.shape, 1) % LANES) < ATT_HEAD_DIM
    v_scr[0] = jnp.where(low_half, vcat, jnp.ones_like(vcat))
    v_scr[1] = jnp.where(low_half, jnp.ones_like(vcat), vcat)

    lane = lax.broadcasted_iota(jnp.int32, (Q_SUB, LANES), 1)
    n_blocks = WINDOW // LANES

    def window(j, h):
        q0 = j * Q_SUB
        w0 = q0
        cols = slice((h // 2) * LANES, (h // 2 + 1) * LANES)
        return q0, w0, cols

    def scores(j, p):
        q0, w0, cols = window(j, 2 * p)
        qp = q_ref[q0:q0 + Q_SUB, cols]
        zero = jnp.zeros_like(qp)
        qz = jnp.concatenate([jnp.where(lane < ATT_HEAD_DIM, qp, zero),
                              jnp.where(lane < ATT_HEAD_DIM, zero, qp)], axis=0)
        s = _dot(qz, kT_scr[cols, w0:w0 + WINDOW])
        heads = []
        for hh in range(2):
            h = 2 * p + hh
            blocks = []
            for b in range(n_blocks):
                before_start = (b + 1) * LANES <= PREV_KEYS - q0
                entry = jnp.where(first_tile, ATT_HEADS, h) if before_start else h
                blk = slice(b * LANES, (b + 1) * LANES)
                blocks.append(s[hh * Q_SUB:(hh + 1) * Q_SUB, blk] + bias_ref[entry, :, blk])
            heads.append(jnp.concatenate(blocks, axis=1))
        return heads

    def attend(j, h, s):
        _, w0, cols = window(j, h)
        p = jnp.exp2(s - jnp.max(s, axis=-1, keepdims=True)).astype(BF16)
        o = _dot(p, v_scr[h % 2, w0:w0 + WINDOW, cols])
        return o / pltpu.roll(o, ATT_HEAD_DIM, axis=1)

    steps = [(j, p) for j in range(tm // Q_SUB) for p in range(ATT_HEADS // 2)]
    pending = [scores(*st) for st in steps[:SCORE_LOOKAHEAD]]
    for k, (j, p) in enumerate(steps):
        if k + SCORE_LOOKAHEAD < len(steps):
            pending.append(scores(*steps[k + SCORE_LOOKAHEAD]))
        s_even, s_odd = pending.pop(0)
        o_even = attend(j, 2 * p, s_even)
        o_odd = attend(j, 2 * p + 1, s_odd)
        q0, _, cols = window(j, 2 * p)
        y = jnp.where(lane < ATT_HEAD_DIM, o_even, o_odd)
        yatt_scr[q0:q0 + Q_SUB, cols] = y.astype(BF16)

    att = _dot(yatt_scr[...], wbatt_ref[...])
    merged = ga_ref[...].astype(F32) * att + mb_ref[...].astype(F32)
    x1 = x_ref[...] + _dot(merged.astype(BF16), wout_ref[...])

    hx = _rmsnorm(x1, gx_ref[...]).astype(BF16)
    xd = d // XATT_HEADS
    qx = (_dot(hx, wxq_ref[...]) * (xd ** -0.5)).astype(BF16)
    for h in range(XATT_HEADS):
        cols = slice(h * xd, (h + 1) * xd)
        s = _dot(qx[:, cols], kmT_ref[0, cols, :])
        e = jnp.exp(s - jnp.max(s, axis=-1, keepdims=True))
        den = jnp.sum(e, axis=-1, keepdims=True)
        xo_scr[:, cols] = (_dot(e.astype(BF16), vm_ref[0, :, cols]) / den).astype(BF16)
    o_ref[...] = x1 + _dot(xo_scr[...], wxo_ref[...])


def _attn(x, q, kT, v, ga, mb, bias, wbatt, wout, gx, wxq, kmT, vm, wxo, seq):
    n, d = x.shape
    tm = TOKEN_TILE
    tps = seq // tm
    mem_len = vm.shape[1]
    tile = lambda w: pl.BlockSpec((tm, w), lambda i: (i, 0))
    prev = lambda i: jnp.maximum(i * (tm // PREV_KEYS) - 1, 0)
    return pl.pallas_call(
        functools.partial(_attn_kernel, tps),
        grid=(n // tm,),
        out_shape=jax.ShapeDtypeStruct((n, d), F32),
        in_specs=[tile(d), tile(ATT_WIDTH),
                  pl.BlockSpec((ATT_WIDTH, PREV_KEYS), lambda i: (0, prev(i))),
                  pl.BlockSpec((ATT_WIDTH, tm), lambda i: (0, i)),
                  pl.BlockSpec((PREV_KEYS, ATT_WIDTH), lambda i: (prev(i), 0)),
                  tile(ATT_WIDTH), tile(d), tile(d),
                  _const_spec(bias.shape), _const_spec(wbatt.shape), _const_spec(wout.shape),
                  _const_spec((1, d)), _const_spec(wxq.shape),
                  pl.BlockSpec((1, d, mem_len), lambda i: (i // tps, 0, 0)),
                  pl.BlockSpec((1, mem_len, d), lambda i: (i // tps, 0, 0)),
                  _const_spec(wxo.shape)],
        out_specs=tile(d),
        scratch_shapes=[pltpu.VMEM((ATT_WIDTH, PREV_KEYS + tm), BF16),
                        pltpu.VMEM((2, PREV_KEYS + tm, ATT_WIDTH), BF16),
                        pltpu.VMEM((tm, ATT_WIDTH), BF16),
                        pltpu.VMEM((tm, d), BF16)],
        compiler_params=_params(),
        name="attn_merge_xattn",
    )(x, q, kT, kT, v, v, ga, mb, bias, wbatt, wout, gx, wxq, kmT, vm, wxo)


def _ffn_kernel(x_ref, g_ref, wfi_ref, wfo_ref, gfin_ref, o_ref, a_scr):
    d_ff = wfo_ref.shape[0]
    subs = [slice(r, r + FFN_SUB) for r in range(0, x_ref.shape[0], FFN_SUB)]
    hb_next = _rmsnorm(x_ref[subs[0], :], g_ref[...]).astype(BF16)
    for r, rows in enumerate(subs):
        hb = hb_next
        for c in range(d_ff // FF_CHUNK):
            gate = _dot(hb, wfi_ref[:, c * FF_CHUNK:(c + 1) * FF_CHUNK])
            up = _dot(hb, wfi_ref[:, d_ff + c * FF_CHUNK:d_ff + (c + 1) * FF_CHUNK])
            a_scr[rows, c * FF_CHUNK:(c + 1) * FF_CHUNK] = (jax.nn.silu(gate) * up).astype(BF16)
            if c == 0 and r + 1 < len(subs):
                hb_next = _rmsnorm(x_ref[subs[r + 1], :], g_ref[...]).astype(BF16)
        x3 = x_ref[rows, :] + _dot(a_scr[rows, :], wfo_ref[...])
        o_ref[rows, :] = _rmsnorm(x3, gfin_ref[...])


def _ffn(x, g, wfi, wfo, gfin):
    n, d = x.shape
    tm = FFN_TILE
    d_ff = wfo.shape[0]
    assert d_ff % FF_CHUNK == 0
    tile = pl.BlockSpec((tm, d), lambda i: (i, 0))
    return pl.pallas_call(
        _ffn_kernel,
        grid=(n // tm,),
        out_shape=jax.ShapeDtypeStruct((n, d), F32),
        in_specs=[tile, _const_spec((1, d)), _const_spec(wfi.shape), _const_spec(wfo.shape),
                  _const_spec((1, d))],
        out_specs=tile,
        scratch_shapes=[pltpu.VMEM((tm, d_ff), BF16)],
        compiler_params=_params(),
        name="ffn_final_norm",
    )(x, g, wfi, wfo, gfin)


def kernel(x, mem, norm_mix_g, w_in, rel_bias, sg_ln_g, sg_ln_b, sg_w, sg_b, w_branch_att,
           w_branch_sg, w_out, norm_xattn_g, norm_mem_g, w_xq, w_xkv, w_xo, norm_ffn_g,
           w_ffn_in, w_ffn_out, norm_final_g):
    b, s, d = x.shape
    depth = w_in.shape[0]
    assert s % TOKEN_TILE == 0 and TOKEN_TILE % PREV_KEYS == 0 and TOKEN_TILE % Q_SUB == 0
    assert (b * s) % FFN_TILE == 0 and (b * s) % INPROJ_TILE == 0
    row = lambda a: a.reshape(1, -1).astype(F32)
    grp = np.arange(MXU_WIDTH) // SG_GROUP_DIM
    avg = jnp.asarray((grp[:, None] == grp[None, :]) / SG_GROUP_DIM, BF16)

    xf = x.reshape(b * s, d)
    for l in range(depth):
        bounds = np.cumsum([0, ATT_WIDTH, ATT_WIDTH, ATT_WIDTH, SG_WIDTH, SG_WIDTH, d, d])
        wq, wk, wv, wu, wvs, wga, wgb = (
            w_in[l, :, int(lo):int(hi)].astype(BF16) for lo, hi in zip(bounds[:-1], bounds[1:]))
        sgw = sg_w[l].reshape(SG_GROUPS // 2, 2 * SG_BLOCK, SG_BLOCK)
        sgb = jnp.repeat(sg_b[l].T, SG_GROUP_DIM, axis=1)

        bias = _bias_table(rel_bias[l])
        kmT, vm = _mem_kv(mem, row(norm_mem_g[l]), w_xkv[l, :, :d].T.astype(BF16),
                          w_xkv[l, :, d:].astype(BF16))
        q, kT, v, ga, mb = _inproj(
            xf, row(norm_mix_g[l]), wq, wk, wv, wu, wvs, wga, wgb, avg,
            row(sg_ln_g[l]), row(sg_ln_b[l]), sgw, sgb, w_branch_sg[l].astype(BF16))
        xf = _attn(xf, q, kT, v, ga, mb, bias, w_branch_att[l].astype(BF16),
                   w_out[l].astype(BF16), row(norm_xattn_g[l]), w_xq[l].astype(BF16),
                   kmT, vm, w_xo[l].astype(BF16), s)
        assert depth == 1
        xf = _ffn(xf, row(norm_ffn_g[l]), w_ffn_in[l].astype(BF16), w_ffn_out[l].astype(BF16),
                  row(norm_final_g))
    return xf.reshape(b, s, d)
```

```python
import functools

import jax
import jax.numpy as jnp
import numpy as np
from jax import lax
from jax.experimental import pallas as pl
from jax.experimental.pallas import tpu as pltpu

CHUNK = 64
N_PREV_CHUNKS = 8
ATT_HEADS = 8
ATT_HEAD_DIM = 64
ATT_WIDTH = ATT_HEADS * ATT_HEAD_DIM
REL_CLIP = 128
SG_BLOCK = 128
SG_GROUPS = 8
SG_GROUP_DIM = 64
SG_WIDTH = SG_GROUPS * SG_GROUP_DIM
XATT_HEADS = 4
EPS = 1e-6
NEG_INF = -1e30
LOG2E = 1.4426950408889634

LANES = 128
MXU_WIDTH = 256
VMEM_LIMIT_BYTES = 62 * 1024 * 1024

TOKEN_TILE = 1024
INPROJ_TILE = 1024
FFN_TILE = 1024
FFN_SUB = 512
Q_SUB = 2 * CHUNK
PREV_KEYS = N_PREV_CHUNKS * CHUNK
WINDOW = PREV_KEYS + Q_SUB
FF_CHUNK = 256
SCORE_LOOKAHEAD = 2

F32 = jnp.float32
BF16 = jnp.bfloat16
_NT = (((1,), (1,)), ((), ()))


def _dot(a, b):
    return jnp.dot(a, b, preferred_element_type=F32)


def _dot_nt(a, b):
    return lax.dot_general(a, b, _NT, preferred_element_type=F32)


def _rmsnorm(x, g):
    return x * lax.rsqrt(jnp.mean(x * x, axis=-1, keepdims=True) + EPS) * g


def _const_spec(shape):
    zeros = (0,) * len(shape)
    return pl.BlockSpec(shape, lambda *_: zeros, pipeline_mode=pl.Buffered(1))


def _params():
    return pltpu.CompilerParams(dimension_semantics=("arbitrary",),
                                vmem_limit_bytes=VMEM_LIMIT_BYTES)


def _bias_kernel(rb_ref, o_ref):
    near_cols = Q_SUB + REL_CLIP
    far_cols = WINDOW - near_cols
    width = near_cols + Q_SUB
    c = lax.broadcasted_iota(jnp.int32, (1, width), 1)
    idx = jnp.clip(near_cols - c, -REL_CLIP, REL_CLIP) + REL_CLIP
    diag = jnp.zeros((ATT_HEADS, width), F32)
    for j in range(2 * REL_CLIP + 1):
        diag = jnp.where(idx == j, rb_ref[:, j:j + 1], diag)
    diag = diag * LOG2E

    row = lax.broadcasted_iota(jnp.int32, (Q_SUB, width), 0)
    qc = lax.broadcasted_iota(jnp.int32, (Q_SUB, WINDOW), 0) // CHUNK
    kc = lax.broadcasted_iota(jnp.int32, (Q_SUB, WINDOW), 1) // CHUNK
    valid = (kc >= qc) & (kc <= qc + N_PREV_CHUNKS)
    for h in range(ATT_HEADS):
        t = jnp.broadcast_to(diag[h:h + 1, :], (Q_SUB, width))
        for bit in range(Q_SUB.bit_length() - 1):
            t = jnp.where(((row >> bit) & 1) == 1, pltpu.roll(t, 1 << bit, axis=1), t)
        far = jnp.broadcast_to(diag[h:h + 1, 0:1], (Q_SUB, far_cols))
        o_ref[h] = jnp.where(valid, jnp.concatenate([far, t[:, Q_SUB:]], axis=1), NEG_INF)
    o_ref[ATT_HEADS] = jnp.full((Q_SUB, WINDOW), NEG_INF, F32)


def _bias_table(rel_bias):
    return pl.pallas_call(
        _bias_kernel,
        out_shape=jax.ShapeDtypeStruct((ATT_HEADS + 1, Q_SUB, WINDOW), F32),
        in_specs=[pl.BlockSpec(memory_space=pltpu.VMEM)],
        out_specs=pl.BlockSpec(memory_space=pltpu.VMEM),
        name="bias_table",
    )(rel_bias)


def _memkv_kernel(mem_ref, g_ref, wkT_ref, wv_ref, kT_ref, v_ref):
    mn = _rmsnorm(mem_ref[0], g_ref[...]).astype(BF16)
    kT_ref[0] = _dot_nt(wkT_ref[...], mn).astype(BF16)
    v_ref[0] = _dot(mn, wv_ref[...]).astype(BF16)


def _mem_kv(mem, g, wkT, wv):
    b, m, d = mem.shape
    return pl.pallas_call(
        _memkv_kernel,
        grid=(b,),
        out_shape=(jax.ShapeDtypeStruct((b, d, m), BF16), jax.ShapeDtypeStruct((b, m, d), BF16)),
        in_specs=[pl.BlockSpec((1, m, d), lambda i: (i, 0, 0)),
                  _const_spec((1, d)), _const_spec((d, d)), _const_spec((d, d))],
        out_specs=(pl.BlockSpec((1, d, m), lambda i: (i, 0, 0)),
                   pl.BlockSpec((1, m, d), lambda i: (i, 0, 0))),
        compiler_params=_params(),
        name="mem_kv",
    )(mem, g, wkT, wv)


def _inproj_kernel(x_ref, g_ref, wq_ref, wk_ref, wv_ref, wu_ref, wvs_ref, wga_ref, wgb_ref,
                   avg_ref, lng_ref, lnb_ref, sgw_ref, sgb_ref, wbsg_ref,
                   q_ref, kT_ref, v_ref, ga_ref, mb_ref, wm_scr, ysg_scr):
    pair_rows = 2 * SG_BLOCK

    @pl.when(pl.program_id(0) == 0)
    def _():
        t = lax.broadcasted_iota(jnp.int32, (pair_rows, SG_BLOCK), 0) % SG_BLOCK
        s = lax.broadcasted_iota(jnp.int32, (pair_rows, SG_BLOCK), 1)
        mask = (s // CHUNK) <= (t // CHUNK)
        for j in range(SG_GROUPS // 2):
            wm_scr[j] = jnp.where(mask, sgw_ref[j], 0.0).astype(BF16)

    hb = _rmsnorm(x_ref[...], g_ref[...]).astype(BF16)
    vs_raw = _dot(hb, wvs_ref[...])
    u_raw = _dot(hb, wu_ref[...])
    q_ref[...] = (_dot(hb, wq_ref[...]) * (ATT_HEAD_DIM ** -0.5 * LOG2E)).astype(BF16)
    vs = jax.nn.gelu(vs_raw)

    def group_mean(t):
        tb = t.astype(BF16)
        w = avg_ref.shape[0]
        return jnp.concatenate([_dot(tb[:, c:c + w], avg_ref[...])
                                for c in range(0, SG_WIDTH, w)], axis=1)

    mean = group_mean(vs)
    kT_ref[...] = _dot(hb, wk_ref[...]).T.astype(BF16)
    dev = vs - mean
    var = group_mean(dev * dev)
    v_ref[...] = _dot(hb, wv_ref[...]).astype(BF16)
    ga_raw = _dot(hb, wga_ref[...])
    vln = (dev * lax.rsqrt(var + EPS) * lng_ref[...] + lnb_ref[...]).astype(BF16)
    u = jax.nn.gelu(u_raw)

    lane = lax.broadcasted_iota(jnp.int32, (SG_BLOCK, LANES), 1)
    for n in range(x_ref.shape[0] // SG_BLOCK):
        rows = slice(n * SG_BLOCK, (n + 1) * SG_BLOCK)
        for j in range(SG_GROUPS // 2):
            cols = slice(j * LANES, (j + 1) * LANES)
            r = _dot(wm_scr[j], vln[rows, cols])
            sv = jnp.where(lane < SG_GROUP_DIM, r[:SG_BLOCK], r[SG_BLOCK:]) + sgb_ref[:, cols]
            ysg_scr[rows, cols] = (u[rows, cols] * sv).astype(BF16)

    gate_b = jax.nn.sigmoid(_dot(hb, wgb_ref[...]))
    mb_ref[...] = (gate_b * _dot(ysg_scr[...], wbsg_ref[...])).astype(BF16)
    ga_ref[...] = jax.nn.sigmoid(ga_raw).astype(BF16)


def _inproj(x, g, wq, wk, wv, wu, wvs, wga, wgb, avg, lng, lnb, sgw, sgb, wbsg):
    n, d = x.shape
    tm = INPROJ_TILE
    assert tm % SG_BLOCK == 0
    tile = lambda w: pl.BlockSpec((tm, w), lambda i: (i, 0))
    return pl.pallas_call(
        _inproj_kernel,
        grid=(n // tm,),
        out_shape=(jax.ShapeDtypeStruct((n, ATT_WIDTH), BF16),
                   jax.ShapeDtypeStruct((ATT_WIDTH, n), BF16),
                   jax.ShapeDtypeStruct((n, ATT_WIDTH), BF16),
                   jax.ShapeDtypeStruct((n, d), BF16),
                   jax.ShapeDtypeStruct((n, d), BF16)),
        in_specs=[tile(d), _const_spec((1, d)),
                  _const_spec(wq.shape), _const_spec(wk.shape), _const_spec(wv.shape),
                  _const_spec(wu.shape), _const_spec(wvs.shape), _const_spec(wga.shape),
                  _const_spec(wgb.shape), _const_spec(avg.shape), _const_spec(lng.shape),
                  _const_spec(lnb.shape), _const_spec(sgw.shape), _const_spec(sgb.shape),
                  _const_spec(wbsg.shape)],
        out_specs=(tile(ATT_WIDTH), pl.BlockSpec((ATT_WIDTH, tm), lambda i: (0, i)),
                   tile(ATT_WIDTH), tile(d), tile(d)),
        scratch_shapes=[pltpu.VMEM((SG_GROUPS // 2, 2 * SG_BLOCK, SG_BLOCK), BF16),
                        pltpu.VMEM((tm, SG_WIDTH), BF16)],
        compiler_params=_params(),
        name="inproj_sgu",
    )(x, g, wq, wk, wv, wu, wvs, wga, wgb, avg, lng, lnb, sgw, sgb, wbsg)


def _attn_kernel(tiles_per_seq, x_ref, q_ref, kTp_ref, kTc_ref, vp_ref, vc_ref, ga_ref, mb_ref,
                 bias_ref, wbatt_ref, wout_ref, gx_ref, wxq_ref, kmT_ref, vm_ref, wxo_ref,
                 o_ref, kT_scr, v_scr, yatt_scr, xo_scr):
    tm = x_ref.shape[0]
    d = x_ref.shape[1]
    first_tile = (pl.program_id(0) % tiles_per_seq) == 0

    kT_scr[:, :PREV_KEYS] = kTp_ref[...]
    kT_scr[:, PREV_KEYS:] = kTc_ref[...]
    vcat = jnp.concatenate([vp_ref[...], vc_ref[...]], axis=0)
    low_half = (lax.broadcasted_iota(jnp.int32, vcat.shape, 1) % LANES) < ATT_HEAD_DIM
    v_scr[0] = jnp.where(low_half, vcat, jnp.ones_like(vcat))
    v_scr[1] = jnp.where(low_half, jnp.ones_like(vcat), vcat)

    lane = lax.broadcasted_iota(jnp.int32, (Q_SUB, LANES), 1)
    n_blocks = WINDOW // LANES

    def window(j, h):
        q0 = j * Q_SUB
        w0 = q0
        cols = slice((h // 2) * LANES, (h // 2 + 1) * LANES)
        return q0, w0, cols

    def scores(j, p):
        q0, w0, cols = window(j, 2 * p)
        qp = q_ref[q0:q0 + Q_SUB, cols]
        zero = jnp.zeros_like(qp)
        qz = jnp.concatenate([jnp.where(lane < ATT_HEAD_DIM, qp, zero),
                              jnp.where(lane < ATT_HEAD_DIM, zero, qp)], axis=0)
        s = _dot(qz, kT_scr[cols, w0:w0 + WINDOW])
        heads = []
        for hh in range(2):
            h = 2 * p + hh
            blocks = []
            for b in range(n_blocks):
                before_start = (b + 1) * LANES <= PREV_KEYS - q0
                entry = jnp.where(first_tile, ATT_HEADS, h) if before_start else h
                blk = slice(b * LANES, (b + 1) * LANES)
                blocks.append(s[hh * Q_SUB:(hh + 1) * Q_SUB, blk] + bias_ref[entry, :, blk])
            heads.append(jnp.concatenate(blocks, axis=1))
        return heads

    def attend(j, h, s):
        _, w0, cols = window(j, h)
        p = jnp.exp2(s - jnp.max(s, axis=-1, keepdims=True)).astype(BF16)
        return _dot(p, v_scr[h % 2, w0:w0 + WINDOW, cols])

    steps = [(j, p) for j in range(tm // Q_SUB) for p in range(ATT_HEADS // 2)]
    pending = [scores(*st) for st in steps[:SCORE_LOOKAHEAD]]
    for k, (j, p) in enumerate(steps):
        if k + SCORE_LOOKAHEAD < len(steps):
            pending.append(scores(*steps[k + SCORE_LOOKAHEAD]))
        s_even, s_odd = pending.pop(0)
        o_even = attend(j, 2 * p, s_even)
        o_odd = attend(j, 2 * p + 1, s_odd)
        q0, _, cols = window(j, 2 * p)
        num = jnp.where(lane < ATT_HEAD_DIM, o_even, o_odd)
        den = jnp.where(lane < ATT_HEAD_DIM, o_odd, o_even)
        y = num / pltpu.roll(den, ATT_HEAD_DIM, axis=1)
        yatt_scr[q0:q0 + Q_SUB, cols] = y.astype(BF16)

    att = _dot(yatt_scr[...], wbatt_ref[...])
    merged = ga_ref[...].astype(F32) * att + mb_ref[...].astype(F32)
    x1 = x_ref[...] + _dot(merged.astype(BF16), wout_ref[...])

    hx = _rmsnorm(x1, gx_ref[...]).astype(BF16)
    xd = d // XATT_HEADS
    qx = (_dot(hx, wxq_ref[...]) * (xd ** -0.5)).astype(BF16)
    for h in range(XATT_HEADS):
        cols = slice(h * xd, (h + 1) * xd)
        s = _dot(qx[:, cols], kmT_ref[0, cols, :])
        e = jnp.exp(s - jnp.max(s, axis=-1, keepdims=True))
        den = jnp.sum(e, axis=-1, keepdims=True)
        xo_scr[:, cols] = (_dot(e.astype(BF16), vm_ref[0, :, cols]) / den).astype(BF16)
    o_ref[...] = x1 + _dot(xo_scr[...], wxo_ref[...])


def _attn(x, q, kT, v, ga, mb, bias, wbatt, wout, gx, wxq, kmT, vm, wxo, seq):
    n, d = x.shape
    tm = TOKEN_TILE
    tps = seq // tm
    mem_len = vm.shape[1]
    tile = lambda w: pl.BlockSpec((tm, w), lambda i: (i, 0))
    prev = lambda i: jnp.maximum(i * (tm // PREV_KEYS) - 1, 0)
    return pl.pallas_call(
        functools.partial(_attn_kernel, tps),
        grid=(n // tm,),
        out_shape=jax.ShapeDtypeStruct((n, d), F32),
        in_specs=[tile(d), tile(ATT_WIDTH),
                  pl.BlockSpec((ATT_WIDTH, PREV_KEYS), lambda i: (0, prev(i))),
                  pl.BlockSpec((ATT_WIDTH, tm), lambda i: (0, i)),
                  pl.BlockSpec((PREV_KEYS, ATT_WIDTH), lambda i: (prev(i), 0)),
                  tile(ATT_WIDTH), tile(d), tile(d),
                  _const_spec(bias.shape), _const_spec(wbatt.shape), _const_spec(wout.shape),
                  _const_spec((1, d)), _const_spec(wxq.shape),
                  pl.BlockSpec((1, d, mem_len), lambda i: (i // tps, 0, 0)),
                  pl.BlockSpec((1, mem_len, d), lambda i: (i // tps, 0, 0)),
                  _const_spec(wxo.shape)],
        out_specs=tile(d),
        scratch_shapes=[pltpu.VMEM((ATT_WIDTH, PREV_KEYS + tm), BF16),
                        pltpu.VMEM((2, PREV_KEYS + tm, ATT_WIDTH), BF16),
                        pltpu.VMEM((tm, ATT_WIDTH), BF16),
                        pltpu.VMEM((tm, d), BF16)],
        compiler_params=_params(),
        name="attn_merge_xattn",
    )(x, q, kT, kT, v, v, ga, mb, bias, wbatt, wout, gx, wxq, kmT, vm, wxo)


def _ffn_kernel(x_ref, g_ref, wfi_ref, wfo_ref, gfin_ref, o_ref, a_scr):
    d_ff = wfo_ref.shape[0]
    subs = [slice(r, r + FFN_SUB) for r in range(0, x_ref.shape[0], FFN_SUB)]
    hb_next = _rmsnorm(x_ref[subs[0], :], g_ref[...]).astype(BF16)
    for r, rows in enumerate(subs):
        hb = hb_next
        for c in range(d_ff // FF_CHUNK):
            gate = _dot(hb, wfi_ref[:, c * FF_CHUNK:(c + 1) * FF_CHUNK])
            up = _dot(hb, wfi_ref[:, d_ff + c * FF_CHUNK:d_ff + (c + 1) * FF_CHUNK])
            a_scr[rows, c * FF_CHUNK:(c + 1) * FF_CHUNK] = (jax.nn.silu(gate) * up).astype(BF16)
            if c == 0 and r + 1 < len(subs):
                hb_next = _rmsnorm(x_ref[subs[r + 1], :], g_ref[...]).astype(BF16)
        x3 = x_ref[rows, :] + _dot(a_scr[rows, :], wfo_ref[...])
        o_ref[rows, :] = _rmsnorm(x3, gfin_ref[...])


def _ffn(x, g, wfi, wfo, gfin):
    n, d = x.shape
    tm = FFN_TILE
    d_ff = wfo.shape[0]
    assert d_ff % FF_CHUNK == 0
    tile = pl.BlockSpec((tm, d), lambda i: (i, 0))
    return pl.pallas_call(
        _ffn_kernel,
        grid=(n // tm,),
        out_shape=jax.ShapeDtypeStruct((n, d), F32),
        in_specs=[tile, _const_spec((1, d)), _const_spec(wfi.shape), _const_spec(wfo.shape),
                  _const_spec((1, d))],
        out_specs=tile,
        scratch_shapes=[pltpu.VMEM((tm, d_ff), BF16)],
        compiler_params=_params(),
        name="ffn_final_norm",
    )(x, g, wfi, wfo, gfin)


def kernel(x, mem, norm_mix_g, w_in, rel_bias, sg_ln_g, sg_ln_b, sg_w, sg_b, w_branch_att,
           w_branch_sg, w_out, norm_xattn_g, norm_mem_g, w_xq, w_xkv, w_xo, norm_ffn_g,
           w_ffn_in, w_ffn_out, norm_final_g):
    b, s, d = x.shape
    depth = w_in.shape[0]
    assert s % TOKEN_TILE == 0 and TOKEN_TILE % PREV_KEYS == 0 and TOKEN_TILE % Q_SUB == 0
    assert (b * s) % FFN_TILE == 0 and (b * s) % INPROJ_TILE == 0
    row = lambda a: a.reshape(1, -1).astype(F32)
    grp = np.arange(MXU_WIDTH) // SG_GROUP_DIM
    avg = jnp.asarray((grp[:, None] == grp[None, :]) / SG_GROUP_DIM, BF16)

    xf = x.reshape(b * s, d)
    for l in range(depth):
        bounds = np.cumsum([0, ATT_WIDTH, ATT_WIDTH, ATT_WIDTH, SG_WIDTH, SG_WIDTH, d, d])
        wq, wk, wv, wu, wvs, wga, wgb = (
            w_in[l, :, int(lo):int(hi)].astype(BF16) for lo, hi in zip(bounds[:-1], bounds[1:]))
        sgw = sg_w[l].reshape(SG_GROUPS // 2, 2 * SG_BLOCK, SG_BLOCK)
        sgb = jnp.repeat(sg_b[l].T, SG_GROUP_DIM, axis=1)

        bias = _bias_table(rel_bias[l])
        kmT, vm = _mem_kv(mem, row(norm_mem_g[l]), w_xkv[l, :, :d].T.astype(BF16),
                          w_xkv[l, :, d:].astype(BF16))
        q, kT, v, ga, mb = _inproj(
            xf, row(norm_mix_g[l]), wq, wk, wv, wu, wvs, wga, wgb, avg,
            row(sg_ln_g[l]), row(sg_ln_b[l]), sgw, sgb, w_branch_sg[l].astype(BF16))
        xf = _attn(xf, q, kT, v, ga, mb, bias, w_branch_att[l].astype(BF16),
                   w_out[l].astype(BF16), row(norm_xattn_g[l]), w_xq[l].astype(BF16),
                   kmT, vm, w_xo[l].astype(BF16), s)
        assert depth == 1
        xf = _ffn(xf, row(norm_ffn_g[l]), w_ffn_in[l].astype(BF16), w_ffn_out[l].astype(BF16),
                  row(norm_final_g))
    return xf.reshape(b, s, d)
```

```python
import functools

import jax
import jax.numpy as jnp
import numpy as np
from jax import lax
from jax.experimental import pallas as pl
from jax.experimental.pallas import tpu as pltpu

CHUNK = 64
N_PREV_CHUNKS = 8
ATT_HEADS = 8
ATT_HEAD_DIM = 64
ATT_WIDTH = ATT_HEADS * ATT_HEAD_DIM
REL_CLIP = 128
SG_BLOCK = 128
SG_GROUPS = 8
SG_GROUP_DIM = 64
SG_WIDTH = SG_GROUPS * SG_GROUP_DIM
XATT_HEADS = 4
EPS = 1e-6
NEG_INF = -1e30
LOG2E = 1.4426950408889634

LANES = 128
MXU_WIDTH = 256
VMEM_LIMIT_BYTES = 62 * 1024 * 1024

TOKEN_TILE = 1024
INPROJ_TILE = 1024
FFN_TILE = 1024
FFN_SUB = 512
Q_SUB = 2 * CHUNK
PREV_KEYS = N_PREV_CHUNKS * CHUNK
WINDOW = PREV_KEYS + Q_SUB
FF_CHUNK = 256
STAGE_BYTES = 3 * 512 * 1024
SCORE_LOOKAHEAD = 2

F32 = jnp.float32
BF16 = jnp.bfloat16
_NT = (((1,), (1,)), ((), ()))


def _dot(a, b):
    return jnp.dot(a, b, preferred_element_type=F32)


def _dot_nt(a, b):
    return lax.dot_general(a, b, _NT, preferred_element_type=F32)


def _rmsnorm(x, g):
    return x * lax.rsqrt(jnp.mean(x * x, axis=-1, keepdims=True) + EPS) * g


def _const_spec(shape):
    zeros = (0,) * len(shape)
    return pl.BlockSpec(shape, lambda *_: zeros, pipeline_mode=pl.Buffered(1))


def _params():
    return pltpu.CompilerParams(dimension_semantics=("arbitrary",),
                                vmem_limit_bytes=VMEM_LIMIT_BYTES)


def _bias_kernel(rb_ref, o_ref):
    near_cols = Q_SUB + REL_CLIP
    far_cols = WINDOW - near_cols
    width = near_cols + Q_SUB
    c = lax.broadcasted_iota(jnp.int32, (1, width), 1)
    idx = jnp.clip(near_cols - c, -REL_CLIP, REL_CLIP) + REL_CLIP
    diag = jnp.zeros((ATT_HEADS, width), F32)
    for j in range(2 * REL_CLIP + 1):
        diag = jnp.where(idx == j, rb_ref[:, j:j + 1], diag)
    diag = diag * LOG2E

    row = lax.broadcasted_iota(jnp.int32, (Q_SUB, width), 0)
    qc = lax.broadcasted_iota(jnp.int32, (Q_SUB, WINDOW), 0) // CHUNK
    kc = lax.broadcasted_iota(jnp.int32, (Q_SUB, WINDOW), 1) // CHUNK
    valid = (kc >= qc) & (kc <= qc + N_PREV_CHUNKS)
    for h in range(ATT_HEADS):
        t = jnp.broadcast_to(diag[h:h + 1, :], (Q_SUB, width))
        for bit in range(Q_SUB.bit_length() - 1):
            t = jnp.where(((row >> bit) & 1) == 1, pltpu.roll(t, 1 << bit, axis=1), t)
        far = jnp.broadcast_to(diag[h:h + 1, 0:1], (Q_SUB, far_cols))
        o_ref[h] = jnp.where(valid, jnp.concatenate([far, t[:, Q_SUB:]], axis=1), NEG_INF)
    o_ref[ATT_HEADS] = jnp.full((Q_SUB, WINDOW), NEG_INF, F32)


def _bias_table(rel_bias):
    return pl.pallas_call(
        _bias_kernel,
        out_shape=jax.ShapeDtypeStruct((ATT_HEADS + 1, Q_SUB, WINDOW), F32),
        in_specs=[pl.BlockSpec(memory_space=pltpu.VMEM)],
        out_specs=pl.BlockSpec(memory_space=pltpu.VMEM),
        name="bias_table",
    )(rel_bias)


def _memkv_kernel(mem_ref, g_ref, wkT_ref, wv_ref, kT_ref, v_ref):
    mn = _rmsnorm(mem_ref[0], g_ref[...]).astype(BF16)
    kT_ref[0] = _dot_nt(wkT_ref[...], mn).astype(BF16)
    v_ref[0] = _dot(mn, wv_ref[...]).astype(BF16)


def _mem_kv(mem, g, wkT, wv):
    b, m, d = mem.shape
    return pl.pallas_call(
        _memkv_kernel,
        grid=(b,),
        out_shape=(jax.ShapeDtypeStruct((b, d, m), BF16), jax.ShapeDtypeStruct((b, m, d), BF16)),
        in_specs=[pl.BlockSpec((1, m, d), lambda i: (i, 0, 0)),
                  _const_spec((1, d)), _const_spec((d, d)), _const_spec((d, d))],
        out_specs=(pl.BlockSpec((1, d, m), lambda i: (i, 0, 0)),
                   pl.BlockSpec((1, m, d), lambda i: (i, 0, 0))),
        compiler_params=_params(),
        name="mem_kv",
    )(mem, g, wkT, wv)


def _inproj_kernel(x_ref, g_ref, wq_ref, wk_ref, wv_ref, wu_ref, wvs_ref, wga_ref, wgb_ref,
                   avg_ref, lng_ref, lnb_ref, sgw_ref, sgb_ref, wbsg_ref,
                   q_ref, kT_ref, v_ref, ga_ref, mb_ref, wm_scr, ysg_scr):
    pair_rows = 2 * SG_BLOCK

    @pl.when(pl.program_id(0) == 0)
    def _():
        t = lax.broadcasted_iota(jnp.int32, (pair_rows, SG_BLOCK), 0) % SG_BLOCK
        s = lax.broadcasted_iota(jnp.int32, (pair_rows, SG_BLOCK), 1)
        mask = (s // CHUNK) <= (t // CHUNK)
        for j in range(SG_GROUPS // 2):
            wm_scr[j] = jnp.where(mask, sgw_ref[j], 0.0).astype(BF16)

    hb = _rmsnorm(x_ref[...], g_ref[...]).astype(BF16)
    vs_raw = _dot(hb, wvs_ref[...])
    u_raw = _dot(hb, wu_ref[...])
    q_ref[...] = (_dot(hb, wq_ref[...]) * (ATT_HEAD_DIM ** -0.5 * LOG2E)).astype(BF16)
    vs = jax.nn.gelu(vs_raw)

    def group_mean(t):
        tb = t.astype(BF16)
        w = avg_ref.shape[0]
        return jnp.concatenate([_dot(tb[:, c:c + w], avg_ref[...])
                                for c in range(0, SG_WIDTH, w)], axis=1)

    mean = group_mean(vs)
    kT_ref[...] = _dot(hb, wk_ref[...]).T.astype(BF16)
    dev = vs - mean
    var = group_mean(dev * dev)
    v_ref[...] = _dot(hb, wv_ref[...]).astype(BF16)
    ga_raw = _dot(hb, wga_ref[...])
    vln = (dev * lax.rsqrt(var + EPS) * lng_ref[...] + lnb_ref[...]).astype(BF16)
    u = jax.nn.gelu(u_raw)

    lane = lax.broadcasted_iota(jnp.int32, (SG_BLOCK, LANES), 1)
    for n in range(x_ref.shape[0] // SG_BLOCK):
        rows = slice(n * SG_BLOCK, (n + 1) * SG_BLOCK)
        for j in range(SG_GROUPS // 2):
            cols = slice(j * LANES, (j + 1) * LANES)
            r = _dot(wm_scr[j], vln[rows, cols])
            sv = jnp.where(lane < SG_GROUP_DIM, r[:SG_BLOCK], r[SG_BLOCK:]) + sgb_ref[:, cols]
            ysg_scr[rows, cols] = (u[rows, cols] * sv).astype(BF16)

    gate_b = jax.nn.sigmoid(_dot(hb, wgb_ref[...]))
    mb_ref[...] = (gate_b * _dot(ysg_scr[...], wbsg_ref[...])).astype(BF16)
    ga_ref[...] = jax.nn.sigmoid(ga_raw).astype(BF16)


def _inproj(x, g, wq, wk, wv, wu, wvs, wga, wgb, avg, lng, lnb, sgw, sgb, wbsg):
    n, d = x.shape
    tm = INPROJ_TILE
    assert tm % SG_BLOCK == 0
    tile = lambda w: pl.BlockSpec((tm, w), lambda i: (i, 0))
    return pl.pallas_call(
        _inproj_kernel,
        grid=(n // tm,),
        out_shape=(jax.ShapeDtypeStruct((n, ATT_WIDTH), BF16),
                   jax.ShapeDtypeStruct((ATT_WIDTH, n), BF16),
                   jax.ShapeDtypeStruct((n, ATT_WIDTH), BF16),
                   jax.ShapeDtypeStruct((n, d), BF16),
                   jax.ShapeDtypeStruct((n, d), BF16)),
        in_specs=[tile(d), _const_spec((1, d)),
                  _const_spec(wq.shape), _const_spec(wk.shape), _const_spec(wv.shape),
                  _const_spec(wu.shape), _const_spec(wvs.shape), _const_spec(wga.shape),
                  _const_spec(wgb.shape), _const_spec(avg.shape), _const_spec(lng.shape),
                  _const_spec(lnb.shape), _const_spec(sgw.shape), _const_spec(sgb.shape),
                  _const_spec(wbsg.shape)],
        out_specs=(tile(ATT_WIDTH), pl.BlockSpec((ATT_WIDTH, tm), lambda i: (0, i)),
                   tile(ATT_WIDTH), tile(d), tile(d)),
        scratch_shapes=[pltpu.VMEM((SG_GROUPS // 2, 2 * SG_BLOCK, SG_BLOCK), BF16),
                        pltpu.VMEM((tm, SG_WIDTH), BF16)],
        compiler_params=_params(),
        name="inproj_sgu",
    )(x, g, wq, wk, wv, wu, wvs, wga, wgb, avg, lng, lnb, sgw, sgb, wbsg)


def _attn_kernel(tiles_per_seq, x_ref, q_ref, kTp_ref, kTc_ref, vp_ref, vc_ref, ga_ref, mb_ref,
                 bias_ref, wbatt_ref, wout_ref, gx_ref, wxq_ref, kmT_ref, vm_ref, wxo_ref,
                 o_ref, kT_scr, v_scr, yatt_scr, xo_scr):
    tm = x_ref.shape[0]
    d = x_ref.shape[1]
    first_tile = (pl.program_id(0) % tiles_per_seq) == 0

    kT_scr[:, :PREV_KEYS] = kTp_ref[...]
    kT_scr[:, PREV_KEYS:] = kTc_ref[...]
    vcat = jnp.concatenate([vp_ref[...], vc_ref[...]], axis=0)
    low_half = (lax.broadcasted_iota(jnp.int32, vcat.shape, 1) % LANES) < ATT_HEAD_DIM
    v_scr[0] = jnp.where(low_half, vcat, jnp.ones_like(vcat))
    v_scr[1] = jnp.where(low_half, jnp.ones_like(vcat), vcat)

    lane = lax.broadcasted_iota(jnp.int32, (Q_SUB, LANES), 1)
    n_blocks = WINDOW // LANES

    def window(j, h):
        q0 = j * Q_SUB
        w0 = q0
        cols = slice((h // 2) * LANES, (h // 2 + 1) * LANES)
        return q0, w0, cols

    def scores(j, p):
        q0, w0, cols = window(j, 2 * p)
        qp = q_ref[q0:q0 + Q_SUB, cols]
        zero = jnp.zeros_like(qp)
        qz = jnp.concatenate([jnp.where(lane < ATT_HEAD_DIM, qp, zero),
                              jnp.where(lane < ATT_HEAD_DIM, zero, qp)], axis=0)
        s = _dot(qz, kT_scr[cols, w0:w0 + WINDOW])
        heads = []
        for hh in range(2):
            h = 2 * p + hh
            blocks = []
            for b in range(n_blocks):
                before_start = (b + 1) * LANES <= PREV_KEYS - q0
                entry = jnp.where(first_tile, ATT_HEADS, h) if before_start else h
                blk = slice(b * LANES, (b + 1) * LANES)
                blocks.append(s[hh * Q_SUB:(hh + 1) * Q_SUB, blk] + bias_ref[entry, :, blk])
            heads.append(jnp.concatenate(blocks, axis=1))
        return heads

    def attend(j, h, s):
        _, w0, cols = window(j, h)
        p = jnp.exp2(s - jnp.max(s, axis=-1, keepdims=True)).astype(BF16)
        o = _dot(p, v_scr[h % 2, w0:w0 + WINDOW, cols])
        return o / pltpu.roll(o, ATT_HEAD_DIM, axis=1)

    steps = [(j, p) for j in range(tm // Q_SUB) for p in range(ATT_HEADS // 2)]
    pending = [scores(*st) for st in steps[:SCORE_LOOKAHEAD]]
    for k, (j, p) in enumerate(steps):
        if k + SCORE_LOOKAHEAD < len(steps):
            pending.append(scores(*steps[k + SCORE_LOOKAHEAD]))
        s_even, s_odd = pending.pop(0)
        o_even = attend(j, 2 * p, s_even)
        o_odd = attend(j, 2 * p + 1, s_odd)
        q0, _, cols = window(j, 2 * p)
        y = jnp.where(lane < ATT_HEAD_DIM, o_even, o_odd)
        yatt_scr[q0:q0 + Q_SUB, cols] = y.astype(BF16)

    att = _dot(yatt_scr[...], wbatt_ref[...])
    merged = ga_ref[...].astype(F32) * att + mb_ref[...].astype(F32)
    x1 = x_ref[...] + _dot(merged.astype(BF16), wout_ref[...])

    hx = _rmsnorm(x1, gx_ref[...]).astype(BF16)
    xd = d // XATT_HEADS
    qx = (_dot(hx, wxq_ref[...]) * (xd ** -0.5)).astype(BF16)
    for h in range(XATT_HEADS):
        cols = slice(h * xd, (h + 1) * xd)
        s = _dot(qx[:, cols], kmT_ref[0, cols, :])
        e = jnp.exp(s - jnp.max(s, axis=-1, keepdims=True))
        den = jnp.sum(e, axis=-1, keepdims=True)
        xo_scr[:, cols] = (_dot(e.astype(BF16), vm_ref[0, :, cols]) / den).astype(BF16)
    o_ref[...] = x1 + _dot(xo_scr[...], wxo_ref[...])


def _attn(x, q, kT, v, ga, mb, bias, wbatt, wout, gx, wxq, kmT, vm, wxo, seq):
    n, d = x.shape
    tm = TOKEN_TILE
    tps = seq // tm
    mem_len = vm.shape[1]
    tile = lambda w: pl.BlockSpec((tm, w), lambda i: (i, 0))
    prev = lambda i: jnp.maximum(i * (tm // PREV_KEYS) - 1, 0)
    return pl.pallas_call(
        functools.partial(_attn_kernel, tps),
        grid=(n // tm,),
        out_shape=jax.ShapeDtypeStruct((n, d), F32),
        in_specs=[tile(d), tile(ATT_WIDTH),
                  pl.BlockSpec((ATT_WIDTH, PREV_KEYS), lambda i: (0, prev(i))),
                  pl.BlockSpec((ATT_WIDTH, tm), lambda i: (0, i)),
                  pl.BlockSpec((PREV_KEYS, ATT_WIDTH), lambda i: (prev(i), 0)),
                  tile(ATT_WIDTH), tile(d), tile(d),
                  _const_spec(bias.shape), _const_spec(wbatt.shape), _const_spec(wout.shape),
                  _const_spec((1, d)), _const_spec(wxq.shape),
                  pl.BlockSpec((1, d, mem_len), lambda i: (i // tps, 0, 0)),
                  pl.BlockSpec((1, mem_len, d), lambda i: (i // tps, 0, 0)),
                  _const_spec(wxo.shape)],
        out_specs=tile(d),
        scratch_shapes=[pltpu.VMEM((ATT_WIDTH, PREV_KEYS + tm), BF16),
                        pltpu.VMEM((2, PREV_KEYS + tm, ATT_WIDTH), BF16),
                        pltpu.VMEM((tm, ATT_WIDTH), BF16),
                        pltpu.VMEM((tm, d), BF16)],
        compiler_params=_params(),
        name="attn_merge_xattn",
    )(x, q, kT, kT, v, v, ga, mb, bias, wbatt, wout, gx, wxq, kmT, vm, wxo)


def _stage_rows(shape):
    n_rows, n_cols = shape
    return max(r for r in range(8, n_rows + 1, 8)
               if n_rows % r == 0 and r * n_cols * 4 <= STAGE_BYTES)


def _stream_cast(src_ref, dst_ref, stage_ref, sem_ref):
    rows = stage_ref.shape[1]
    n = dst_ref.shape[0] // rows

    def copy(c):
        return pltpu.make_async_copy(src_ref.at[c * rows:(c + 1) * rows], stage_ref.at[c % 2],
                                     sem_ref.at[c % 2])

    copy(0).start()
    for c in range(n):
        if c + 1 < n:
            copy(c + 1).start()
        copy(c).wait()
        dst_ref[c * rows:(c + 1) * rows, :] = stage_ref[c % 2].astype(BF16)


def _ffn_kernel(x_ref, g_ref, wfi_hbm, wfo_hbm, gfin_ref, o_ref,
                a_scr, wfi_ref, wfo_ref, stage_in, stage_out, sem_in, sem_out):
    d_ff = wfo_ref.shape[0]

    @pl.when(pl.program_id(0) == 0)
    def _():
        _stream_cast(wfi_hbm, wfi_ref, stage_in, sem_in)
        _stream_cast(wfo_hbm, wfo_ref, stage_out, sem_out)

    subs = [slice(r, r + FFN_SUB) for r in range(0, x_ref.shape[0], FFN_SUB)]
    hb_next = _rmsnorm(x_ref[subs[0], :], g_ref[...]).astype(BF16)
    for r, rows in enumerate(subs):
        hb = hb_next
        for c in range(d_ff // FF_CHUNK):
            gate = _dot(hb, wfi_ref[:, c * FF_CHUNK:(c + 1) * FF_CHUNK])
            up = _dot(hb, wfi_ref[:, d_ff + c * FF_CHUNK:d_ff + (c + 1) * FF_CHUNK])
            a_scr[rows, c * FF_CHUNK:(c + 1) * FF_CHUNK] = (jax.nn.silu(gate) * up).astype(BF16)
            if c == 0 and r + 1 < len(subs):
                hb_next = _rmsnorm(x_ref[subs[r + 1], :], g_ref[...]).astype(BF16)
        x3 = x_ref[rows, :] + _dot(a_scr[rows, :], wfo_ref[...])
        o_ref[rows, :] = _rmsnorm(x3, gfin_ref[...])


def _ffn(x, g, wfi, wfo, gfin):
    n, d = x.shape
    tm = FFN_TILE
    d_ff = wfo.shape[0]
    assert d_ff % FF_CHUNK == 0
    tile = pl.BlockSpec((tm, d), lambda i: (i, 0))
    hbm = pl.BlockSpec(memory_space=pl.ANY)
    return pl.pallas_call(
        _ffn_kernel,
        grid=(n // tm,),
        out_shape=jax.ShapeDtypeStruct((n, d), F32),
        in_specs=[tile, _const_spec((1, d)), hbm, hbm, _const_spec((1, d))],
        out_specs=tile,
        scratch_shapes=[pltpu.VMEM((tm, d_ff), BF16),
                        pltpu.VMEM(wfi.shape, BF16), pltpu.VMEM(wfo.shape, BF16),
                        pltpu.VMEM((2, _stage_rows(wfi.shape), wfi.shape[1]), F32),
                        pltpu.VMEM((2, _stage_rows(wfo.shape), wfo.shape[1]), F32),
                        pltpu.SemaphoreType.DMA((2,)), pltpu.SemaphoreType.DMA((2,))],
        compiler_params=_params(),
        name="ffn_final_norm",
    )(x, g, wfi, wfo, gfin)


def kernel(x, mem, norm_mix_g, w_in, rel_bias, sg_ln_g, sg_ln_b, sg_w, sg_b, w_branch_att,
           w_branch_sg, w_out, norm_xattn_g, norm_mem_g, w_xq, w_xkv, w_xo, norm_ffn_g,
           w_ffn_in, w_ffn_out, norm_final_g):
    b, s, d = x.shape
    depth = w_in.shape[0]
    assert s % TOKEN_TILE == 0 and TOKEN_TILE % PREV_KEYS == 0 and TOKEN_TILE % Q_SUB == 0
    assert (b * s) % FFN_TILE == 0 and (b * s) % INPROJ_TILE == 0
    row = lambda a: a.reshape(1, -1).astype(F32)
    grp = np.arange(MXU_WIDTH) // SG_GROUP_DIM
    avg = jnp.asarray((grp[:, None] == grp[None, :]) / SG_GROUP_DIM, BF16)

    xf = x.reshape(b * s, d)
    for l in range(depth):
        bounds = np.cumsum([0, ATT_WIDTH, ATT_WIDTH, ATT_WIDTH, SG_WIDTH, SG_WIDTH, d, d])
        wq, wk, wv, wu, wvs, wga, wgb = (
            w_in[l, :, int(lo):int(hi)].astype(BF16) for lo, hi in zip(bounds[:-1], bounds[1:]))
        sgw = sg_w[l].reshape(SG_GROUPS // 2, 2 * SG_BLOCK, SG_BLOCK)
        sgb = jnp.repeat(sg_b[l].T, SG_GROUP_DIM, axis=1)

        bias = _bias_table(rel_bias[l])
        kmT, vm = _mem_kv(mem, row(norm_mem_g[l]), w_xkv[l, :, :d].T.astype(BF16),
                          w_xkv[l, :, d:].astype(BF16))
        q, kT, v, ga, mb = _inproj(
            xf, row(norm_mix_g[l]), wq, wk, wv, wu, wvs, wga, wgb, avg,
            row(sg_ln_g[l]), row(sg_ln_b[l]), sgw, sgb, w_branch_sg[l].astype(BF16))
        xf = _attn(xf, q, kT, v, ga, mb, bias, w_branch_att[l].astype(BF16),
                   w_out[l].astype(BF16), row(norm_xattn_g[l]), w_xq[l].astype(BF16),
                   kmT, vm, w_xo[l].astype(BF16), s)
        assert depth == 1
        xf = _ffn(xf, row(norm_ffn_g[l]), w_ffn_in[l], w_ffn_out[l], row(norm_final_g))
    return xf.reshape(b, s, d)
```

```python
import functools

import jax
import jax.numpy as jnp
import numpy as np
from jax import lax
from jax.experimental import pallas as pl
from jax.experimental.pallas import tpu as pltpu

CHUNK = 64
N_PREV_CHUNKS = 8
ATT_HEADS = 8
ATT_HEAD_DIM = 64
ATT_WIDTH = ATT_HEADS * ATT_HEAD_DIM
REL_CLIP = 128
SG_BLOCK = 128
SG_GROUPS = 8
SG_GROUP_DIM = 64
SG_WIDTH = SG_GROUPS * SG_GROUP_DIM
XATT_HEADS = 4
EPS = 1e-6
NEG_INF = -1e30
LOG2E = 1.4426950408889634

LANES = 128
MXU_WIDTH = 256
VMEM_LIMIT_BYTES = 62 * 1024 * 1024

TOKEN_TILE = 1024
INPROJ_TILE = 1024
FFN_TILE = 1024
FFN_SUB = 512
Q_SUB = 2 * CHUNK
PREV_KEYS = N_PREV_CHUNKS * CHUNK
WINDOW = PREV_KEYS + Q_SUB
FF_CHUNK = 256
SCORE_LOOKAHEAD = 2

F32 = jnp.float32
BF16 = jnp.bfloat16
_NT = (((1,), (1,)), ((), ()))


def _dot(a, b):
    return jnp.dot(a, b, preferred_element_type=F32)


def _dot_nt(a, b):
    return lax.dot_general(a, b, _NT, preferred_element_type=F32)


def _rmsnorm(x, g):
    return x * lax.rsqrt(jnp.mean(x * x, axis=-1, keepdims=True) + EPS) * g


def _const_spec(shape):
    zeros = (0,) * len(shape)
    return pl.BlockSpec(shape, lambda *_: zeros, pipeline_mode=pl.Buffered(1))


def _params():
    return pltpu.CompilerParams(dimension_semantics=("arbitrary",),
                                vmem_limit_bytes=VMEM_LIMIT_BYTES)


def _bias_kernel(rb_ref, o_ref):
    near_cols = Q_SUB + REL_CLIP
    far_cols = WINDOW - near_cols
    width = near_cols + Q_SUB
    c = lax.broadcasted_iota(jnp.int32, (1, width), 1)
    idx = jnp.clip(near_cols - c, -REL_CLIP, REL_CLIP) + REL_CLIP
    diag = jnp.zeros((ATT_HEADS, width), F32)
    for j in range(2 * REL_CLIP + 1):
        diag = jnp.where(idx == j, rb_ref[:, j:j + 1], diag)
    diag = diag * LOG2E

    row = lax.broadcasted_iota(jnp.int32, (Q_SUB, width), 0)
    qc = lax.broadcasted_iota(jnp.int32, (Q_SUB, WINDOW), 0) // CHUNK
    kc = lax.broadcasted_iota(jnp.int32, (Q_SUB, WINDOW), 1) // CHUNK
    valid = (kc >= qc) & (kc <= qc + N_PREV_CHUNKS)
    for h in range(ATT_HEADS):
        t = jnp.broadcast_to(diag[h:h + 1, :], (Q_SUB, width))
        for bit in range(Q_SUB.bit_length() - 1):
            t = jnp.where(((row >> bit) & 1) == 1, pltpu.roll(t, 1 << bit, axis=1), t)
        far = jnp.broadcast_to(diag[h:h + 1, 0:1], (Q_SUB, far_cols))
        o_ref[h] = jnp.where(valid, jnp.concatenate([far, t[:, Q_SUB:]], axis=1), NEG_INF)
    o_ref[ATT_HEADS] = jnp.full((Q_SUB, WINDOW), NEG_INF, F32)


def _bias_table(rel_bias):
    return pl.pallas_call(
        _bias_kernel,
        out_shape=jax.ShapeDtypeStruct((ATT_HEADS + 1, Q_SUB, WINDOW), F32),
        in_specs=[pl.BlockSpec(memory_space=pltpu.VMEM)],
        out_specs=pl.BlockSpec(memory_space=pltpu.VMEM),
        name="bias_table",
    )(rel_bias)


def _memkv_kernel(mem_ref, g_ref, wkT_ref, wv_ref, kT_ref, v_ref):
    mn = _rmsnorm(mem_ref[0], g_ref[...]).astype(BF16)
    kT_ref[0] = _dot_nt(wkT_ref[...], mn).astype(BF16)
    v_ref[0] = _dot(mn, wv_ref[...]).astype(BF16)


def _mem_kv(mem, g, wkT, wv):
    b, m, d = mem.shape
    return pl.pallas_call(
        _memkv_kernel,
        grid=(b,),
        out_shape=(jax.ShapeDtypeStruct((b, d, m), BF16), jax.ShapeDtypeStruct((b, m, d), BF16)),
        in_specs=[pl.BlockSpec((1, m, d), lambda i: (i, 0, 0)),
                  _const_spec((1, d)), _const_spec((d, d)), _const_spec((d, d))],
        out_specs=(pl.BlockSpec((1, d, m), lambda i: (i, 0, 0)),
                   pl.BlockSpec((1, m, d), lambda i: (i, 0, 0))),
        compiler_params=_params(),
        name="mem_kv",
    )(mem, g, wkT, wv)


def _inproj_kernel(x_ref, g_ref, wq_ref, wk_ref, wv_ref, wu_ref, wvs_ref, wga_ref, wgb_ref,
                   avg_ref, lng_ref, lnb_ref, sgw_ref, sgb_ref, wbsg_ref,
                   q_ref, kT_ref, v_ref, ga_ref, mb_ref, wm_scr, ysg_scr):
    pair_rows = 2 * SG_BLOCK

    @pl.when(pl.program_id(0) == 0)
    def _():
        t = lax.broadcasted_iota(jnp.int32, (pair_rows, SG_BLOCK), 0) % SG_BLOCK
        s = lax.broadcasted_iota(jnp.int32, (pair_rows, SG_BLOCK), 1)
        mask = (s // CHUNK) <= (t // CHUNK)
        for j in range(SG_GROUPS // 2):
            wm_scr[j] = jnp.where(mask, sgw_ref[j], 0.0).astype(BF16)

    hb = _rmsnorm(x_ref[...], g_ref[...]).astype(BF16)
    vs_raw = _dot(hb, wvs_ref[...])
    u_raw = _dot(hb, wu_ref[...])
    q_ref[...] = (_dot(hb, wq_ref[...]) * (ATT_HEAD_DIM ** -0.5 * LOG2E)).astype(BF16)
    vs = jax.nn.gelu(vs_raw)

    def group_mean(t):
        tb = t.astype(BF16)
        w = avg_ref.shape[0]
        return jnp.concatenate([_dot(tb[:, c:c + w], avg_ref[...])
                                for c in range(0, SG_WIDTH, w)], axis=1)

    mean = group_mean(vs)
    kT_ref[...] = _dot(hb, wk_ref[...]).T.astype(BF16)
    dev = vs - mean
    var = group_mean(dev * dev)
    v_ref[...] = _dot(hb, wv_ref[...]).astype(BF16)
    ga_raw = _dot(hb, wga_ref[...])
    vln = (dev * lax.rsqrt(var + EPS) * lng_ref[...] + lnb_ref[...]).astype(BF16)
    u = jax.nn.gelu(u_raw)

    lane = lax.broadcasted_iota(jnp.int32, (SG_BLOCK, LANES), 1)
    for n in range(x_ref.shape[0] // SG_BLOCK):
        rows = slice(n * SG_BLOCK, (n + 1) * SG_BLOCK)
        for j in range(SG_GROUPS // 2):
            cols = slice(j * LANES, (j + 1) * LANES)
            r = _dot(wm_scr[j], vln[rows, cols])
            sv = jnp.where(lane < SG_GROUP_DIM, r[:SG_BLOCK], r[SG_BLOCK:]) + sgb_ref[:, cols]
            ysg_scr[rows, cols] = (u[rows, cols] * sv).astype(BF16)

    gate_b = jax.nn.sigmoid(_dot(hb, wgb_ref[...]))
    mb_ref[...] = (gate_b * _dot(ysg_scr[...], wbsg_ref[...])).astype(BF16)
    ga_ref[...] = jax.nn.sigmoid(ga_raw).astype(BF16)


def _inproj(x, g, wq, wk, wv, wu, wvs, wga, wgb, avg, lng, lnb, sgw, sgb, wbsg):
    n, d = x.shape
    tm = INPROJ_TILE
    assert tm % SG_BLOCK == 0
    tile = lambda w: pl.BlockSpec((tm, w), lambda i: (i, 0))
    return pl.pallas_call(
        _inproj_kernel,
        grid=(n // tm,),
        out_shape=(jax.ShapeDtypeStruct((n, ATT_WIDTH), BF16),
                   jax.ShapeDtypeStruct((ATT_WIDTH, n), BF16),
                   jax.ShapeDtypeStruct((n, ATT_WIDTH), BF16),
                   jax.ShapeDtypeStruct((n, d), BF16),
                   jax.ShapeDtypeStruct((n, d), BF16)),
        in_specs=[tile(d), _const_spec((1, d)),
                  _const_spec(wq.shape), _const_spec(wk.shape), _const_spec(wv.shape),
                  _const_spec(wu.shape), _const_spec(wvs.shape), _const_spec(wga.shape),
                  _const_spec(wgb.shape), _const_spec(avg.shape), _const_spec(lng.shape),
                  _const_spec(lnb.shape), _const_spec(sgw.shape), _const_spec(sgb.shape),
                  _const_spec(wbsg.shape)],
        out_specs=(tile(ATT_WIDTH), pl.BlockSpec((ATT_WIDTH, tm), lambda i: (0, i)),
                   tile(ATT_WIDTH), tile(d), tile(d)),
        scratch_shapes=[pltpu.VMEM((SG_GROUPS // 2, 2 * SG_BLOCK, SG_BLOCK), BF16),
                        pltpu.VMEM((tm, SG_WIDTH), BF16)],
        compiler_params=_params(),
        name="inproj_sgu",
    )(x, g, wq, wk, wv, wu, wvs, wga, wgb, avg, lng, lnb, sgw, sgb, wbsg)


def _attn_kernel(tiles_per_seq, x_ref, q_ref, kTp_ref, kTc_ref, vp_ref, vc_ref, ga_ref, mb_ref,
                 bias_ref, wbatt_ref, wout_ref, gx_ref, wxq_ref, kmT_ref, vm_ref, wxo_ref,
                 o_ref, kT_scr, v_scr, yatt_scr, xo_scr):
    tm = x_ref.shape[0]
    d = x_ref.shape[1]
    first_tile = (pl.program_id(0) % tiles_per_seq) == 0

    kcat = jnp.concatenate([kTp_ref[...], kTc_ref[...]], axis=1)
    low_rows = (lax.broadcasted_iota(jnp.int32, kcat.shape, 0) % LANES) < ATT_HEAD_DIM
    kT_scr[0] = jnp.where(low_rows, kcat, jnp.zeros_like(kcat))
    kT_scr[1] = jnp.where(low_rows, jnp.zeros_like(kcat), kcat)
    vcat = jnp.concatenate([vp_ref[...], vc_ref[...]], axis=0)
    low_half = (lax.broadcasted_iota(jnp.int32, vcat.shape, 1) % LANES) < ATT_HEAD_DIM
    v_scr[0] = jnp.where(low_half, vcat, jnp.ones_like(vcat))
    v_scr[1] = jnp.where(low_half, jnp.ones_like(vcat), vcat)

    lane = lax.broadcasted_iota(jnp.int32, (Q_SUB, LANES), 1)
    n_blocks = WINDOW // LANES

    def window(j, h):
        q0 = j * Q_SUB
        w0 = q0
        cols = slice((h // 2) * LANES, (h // 2 + 1) * LANES)
        return q0, w0, cols

    def scores(j, p):
        q0, w0, cols = window(j, 2 * p)
        qp = q_ref[q0:q0 + Q_SUB, cols]
        kw = jnp.concatenate([kT_scr[0, cols, w0:w0 + WINDOW], kT_scr[1, cols, w0:w0 + WINDOW]],
                             axis=1)
        s = _dot(qp, kw)
        heads = []
        for hh in range(2):
            h = 2 * p + hh
            blocks = []
            for b in range(n_blocks):
                before_start = (b + 1) * LANES <= PREV_KEYS - q0
                entry = jnp.where(first_tile, ATT_HEADS, h) if before_start else h
                blk = slice(b * LANES, (b + 1) * LANES)
                sblk = slice(hh * WINDOW + b * LANES, hh * WINDOW + (b + 1) * LANES)
                blocks.append(s[:, sblk] + bias_ref[entry, :, blk])
            heads.append(jnp.concatenate(blocks, axis=1))
        return heads

    def attend(j, h, s):
        _, w0, cols = window(j, h)
        p = jnp.exp2(s - jnp.max(s, axis=-1, keepdims=True)).astype(BF16)
        o = _dot(p, v_scr[h % 2, w0:w0 + WINDOW, cols])
        return o / pltpu.roll(o, ATT_HEAD_DIM, axis=1)

    steps = [(j, p) for j in range(tm // Q_SUB) for p in range(ATT_HEADS // 2)]
    pending = [scores(*st) for st in steps[:SCORE_LOOKAHEAD]]
    for k, (j, p) in enumerate(steps):
        if k + SCORE_LOOKAHEAD < len(steps):
            pending.append(scores(*steps[k + SCORE_LOOKAHEAD]))
        s_even, s_odd = pending.pop(0)
        o_even = attend(j, 2 * p, s_even)
        o_odd = attend(j, 2 * p + 1, s_odd)
        q0, _, cols = window(j, 2 * p)
        y = jnp.where(lane < ATT_HEAD_DIM, o_even, o_odd)
        yatt_scr[q0:q0 + Q_SUB, cols] = y.astype(BF16)

    att = _dot(yatt_scr[...], wbatt_ref[...])
    merged = ga_ref[...].astype(F32) * att + mb_ref[...].astype(F32)
    x1 = x_ref[...] + _dot(merged.astype(BF16), wout_ref[...])

    hx = _rmsnorm(x1, gx_ref[...]).astype(BF16)
    xd = d // XATT_HEADS
    qx = (_dot(hx, wxq_ref[...]) * (xd ** -0.5)).astype(BF16)
    for h in range(XATT_HEADS):
        cols = slice(h * xd, (h + 1) * xd)
        s = _dot(qx[:, cols], kmT_ref[0, cols, :])
        e = jnp.exp(s - jnp.max(s, axis=-1, keepdims=True))
        den = jnp.sum(e, axis=-1, keepdims=True)
        xo_scr[:, cols] = (_dot(e.astype(BF16), vm_ref[0, :, cols]) / den).astype(BF16)
    o_ref[...] = x1 + _dot(xo_scr[...], wxo_ref[...])


def _attn(x, q, kT, v, ga, mb, bias, wbatt, wout, gx, wxq, kmT, vm, wxo, seq):
    n, d = x.shape
    tm = TOKEN_TILE
    tps = seq // tm
    mem_len = vm.shape[1]
    tile = lambda w: pl.BlockSpec((tm, w), lambda i: (i, 0))
    prev = lambda i: jnp.maximum(i * (tm // PREV_KEYS) - 1, 0)
    return pl.pallas_call(
        functools.partial(_attn_kernel, tps),
        grid=(n // tm,),
        out_shape=jax.ShapeDtypeStruct((n, d), F32),
        in_specs=[tile(d), tile(ATT_WIDTH),
                  pl.BlockSpec((ATT_WIDTH, PREV_KEYS), lambda i: (0, prev(i))),
                  pl.BlockSpec((ATT_WIDTH, tm), lambda i: (0, i)),
                  pl.BlockSpec((PREV_KEYS, ATT_WIDTH), lambda i: (prev(i), 0)),
                  tile(ATT_WIDTH), tile(d), tile(d),
                  _const_spec(bias.shape), _const_spec(wbatt.shape), _const_spec(wout.shape),
                  _const_spec((1, d)), _const_spec(wxq.shape),
                  pl.BlockSpec((1, d, mem_len), lambda i: (i // tps, 0, 0)),
                  pl.BlockSpec((1, mem_len, d), lambda i: (i // tps, 0, 0)),
                  _const_spec(wxo.shape)],
        out_specs=tile(d),
        scratch_shapes=[pltpu.VMEM((2, ATT_WIDTH, PREV_KEYS + tm), BF16),
                        pltpu.VMEM((2, PREV_KEYS + tm, ATT_WIDTH), BF16),
                        pltpu.VMEM((tm, ATT_WIDTH), BF16),
                        pltpu.VMEM((tm, d), BF16)],
        compiler_params=_params(),
        name="attn_merge_xattn",
    )(x, q, kT, kT, v, v, ga, mb, bias, wbatt, wout, gx, wxq, kmT, vm, wxo)


def _ffn_kernel(x_ref, g_ref, wfi_ref, wfo_ref, gfin_ref, o_ref, a_scr):
    d_ff = wfo_ref.shape[0]
    subs = [slice(r, r + FFN_SUB) for r in range(0, x_ref.shape[0], FFN_SUB)]
    hb_next = _rmsnorm(x_ref[subs[0], :], g_ref[...]).astype(BF16)
    for r, rows in enumerate(subs):
        hb = hb_next
        for c in range(d_ff // FF_CHUNK):
            gate = _dot(hb, wfi_ref[:, c * FF_CHUNK:(c + 1) * FF_CHUNK])
            up = _dot(hb, wfi_ref[:, d_ff + c * FF_CHUNK:d_ff + (c + 1) * FF_CHUNK])
            a_scr[rows, c * FF_CHUNK:(c + 1) * FF_CHUNK] = (jax.nn.silu(gate) * up).astype(BF16)
            if c == 0 and r + 1 < len(subs):
                hb_next = _rmsnorm(x_ref[subs[r + 1], :], g_ref[...]).astype(BF16)
        x3 = x_ref[rows, :] + _dot(a_scr[rows, :], wfo_ref[...])
        o_ref[rows, :] = _rmsnorm(x3, gfin_ref[...])


def _ffn(x, g, wfi, wfo, gfin):
    n, d = x.shape
    tm = FFN_TILE
    d_ff = wfo.shape[0]
    assert d_ff % FF_CHUNK == 0
    tile = pl.BlockSpec((tm, d), lambda i: (i, 0))
    return pl.pallas_call(
        _ffn_kernel,
        grid=(n // tm,),
        out_shape=jax.ShapeDtypeStruct((n, d), F32),
        in_specs=[tile, _const_spec((1, d)), _const_spec(wfi.shape), _const_spec(wfo.shape),
                  _const_spec((1, d))],
        out_specs=tile,
        scratch_shapes=[pltpu.VMEM((tm, d_ff), BF16)],
        compiler_params=_params(),
        name="ffn_final_norm",
    )(x, g, wfi, wfo, gfin)


def kernel(x, mem, norm_mix_g, w_in, rel_bias, sg_ln_g, sg_ln_b, sg_w, sg_b, w_branch_att,
           w_branch_sg, w_out, norm_xattn_g, norm_mem_g, w_xq, w_xkv, w_xo, norm_ffn_g,
           w_ffn_in, w_ffn_out, norm_final_g):
    b, s, d = x.shape
    depth = w_in.shape[0]
    assert s % TOKEN_TILE == 0 and TOKEN_TILE % PREV_KEYS == 0 and TOKEN_TILE % Q_SUB == 0
    assert (b * s) % FFN_TILE == 0 and (b * s) % INPROJ_TILE == 0
    row = lambda a: a.reshape(1, -1).astype(F32)
    grp = np.arange(MXU_WIDTH) // SG_GROUP_DIM
    avg = jnp.asarray((grp[:, None] == grp[None, :]) / SG_GROUP_DIM, BF16)

    xf = x.reshape(b * s, d)
    for l in range(depth):
        bounds = np.cumsum([0, ATT_WIDTH, ATT_WIDTH, ATT_WIDTH, SG_WIDTH, SG_WIDTH, d, d])
        wq, wk, wv, wu, wvs, wga, wgb = (
            w_in[l, :, int(lo):int(hi)].astype(BF16) for lo, hi in zip(bounds[:-1], bounds[1:]))
        sgw = sg_w[l].reshape(SG_GROUPS // 2, 2 * SG_BLOCK, SG_BLOCK)
        sgb = jnp.repeat(sg_b[l].T, SG_GROUP_DIM, axis=1)

        bias = _bias_table(rel_bias[l])
        kmT, vm = _mem_kv(mem, row(norm_mem_g[l]), w_xkv[l, :, :d].T.astype(BF16),
                          w_xkv[l, :, d:].astype(BF16))
        q, kT, v, ga, mb = _inproj(
            xf, row(norm_mix_g[l]), wq, wk, wv, wu, wvs, wga, wgb, avg,
            row(sg_ln_g[l]), row(sg_ln_b[l]), sgw, sgb, w_branch_sg[l].astype(BF16))
        xf = _attn(xf, q, kT, v, ga, mb, bias, w_branch_att[l].astype(BF16),
                   w_out[l].astype(BF16), row(norm_xattn_g[l]), w_xq[l].astype(BF16),
                   kmT, vm, w_xo[l].astype(BF16), s)
        assert depth == 1
        xf = _ffn(xf, row(norm_ffn_g[l]), w_ffn_in[l].astype(BF16), w_ffn_out[l].astype(BF16),
                  row(norm_final_g))
    return xf.reshape(b, s, d)
```

```python
import functools

import jax
import jax.numpy as jnp
import numpy as np
from jax import lax
from jax.experimental import pallas as pl
from jax.experimental.pallas import tpu as pltpu

CHUNK = 64
N_PREV_CHUNKS = 8
ATT_HEADS = 8
ATT_HEAD_DIM = 64
ATT_WIDTH = ATT_HEADS * ATT_HEAD_DIM
REL_CLIP = 128
SG_BLOCK = 128
SG_GROUPS = 8
SG_GROUP_DIM = 64
SG_WIDTH = SG_GROUPS * SG_GROUP_DIM
XATT_HEADS = 4
EPS = 1e-6
NEG_INF = -1e30
LOG2E = 1.4426950408889634

LANES = 128
MXU_WIDTH = 256
VMEM_LIMIT_BYTES = 62 * 1024 * 1024

TOKEN_TILE = 1024
TAIL_ROWS = 512
INPROJ_TILE = 1024
INPROJ_ROWS = 512
FFN_TILE = 1024
FFN_SUB = 512
Q_SUB = 2 * CHUNK
PREV_KEYS = N_PREV_CHUNKS * CHUNK
WINDOW = PREV_KEYS + Q_SUB
FF_CHUNK = 256
SCORE_LOOKAHEAD = 2

F32 = jnp.float32
BF16 = jnp.bfloat16
_NT = (((1,), (1,)), ((), ()))


def _dot(a, b):
    return jnp.dot(a, b, preferred_element_type=F32)


def _dot_nt(a, b):
    return lax.dot_general(a, b, _NT, preferred_element_type=F32)


def _rmsnorm(x, g):
    return x * lax.rsqrt(jnp.mean(x * x, axis=-1, keepdims=True) + EPS) * g


def _const_spec(shape):
    zeros = (0,) * len(shape)
    return pl.BlockSpec(shape, lambda *_: zeros, pipeline_mode=pl.Buffered(1))


def _params():
    return pltpu.CompilerParams(dimension_semantics=("arbitrary",),
                                vmem_limit_bytes=VMEM_LIMIT_BYTES)


def _bias_kernel(rb_ref, o_ref):
    near_cols = Q_SUB + REL_CLIP
    far_cols = WINDOW - near_cols
    width = near_cols + Q_SUB
    c = lax.broadcasted_iota(jnp.int32, (1, width), 1)
    idx = jnp.clip(near_cols - c, -REL_CLIP, REL_CLIP) + REL_CLIP
    diag = jnp.zeros((ATT_HEADS, width), F32)
    for j in range(2 * REL_CLIP + 1):
        diag = jnp.where(idx == j, rb_ref[:, j:j + 1], diag)
    diag = diag * LOG2E

    row = lax.broadcasted_iota(jnp.int32, (Q_SUB, width), 0)
    qc = lax.broadcasted_iota(jnp.int32, (Q_SUB, WINDOW), 0) // CHUNK
    kc = lax.broadcasted_iota(jnp.int32, (Q_SUB, WINDOW), 1) // CHUNK
    valid = (kc >= qc) & (kc <= qc + N_PREV_CHUNKS)
    for h in range(ATT_HEADS):
        t = jnp.broadcast_to(diag[h:h + 1, :], (Q_SUB, width))
        for bit in range(Q_SUB.bit_length() - 1):
            t = jnp.where(((row >> bit) & 1) == 1, pltpu.roll(t, 1 << bit, axis=1), t)
        far = jnp.broadcast_to(diag[h:h + 1, 0:1], (Q_SUB, far_cols))
        o_ref[h] = jnp.where(valid, jnp.concatenate([far, t[:, Q_SUB:]], axis=1), NEG_INF)
    o_ref[ATT_HEADS] = jnp.full((Q_SUB, WINDOW), NEG_INF, F32)


def _bias_table(rel_bias):
    return pl.pallas_call(
        _bias_kernel,
        out_shape=jax.ShapeDtypeStruct((ATT_HEADS + 1, Q_SUB, WINDOW), F32),
        in_specs=[pl.BlockSpec(memory_space=pltpu.VMEM)],
        out_specs=pl.BlockSpec(memory_space=pltpu.VMEM),
        name="bias_table",
    )(rel_bias)


def _memkv_kernel(mem_ref, g_ref, wkT_ref, wv_ref, kT_ref, v_ref):
    mn = _rmsnorm(mem_ref[0], g_ref[...]).astype(BF16)
    kT_ref[0] = _dot_nt(wkT_ref[...], mn).astype(BF16)
    v_ref[0] = _dot(mn, wv_ref[...]).astype(BF16)


def _mem_kv(mem, g, wkT, wv):
    b, m, d = mem.shape
    return pl.pallas_call(
        _memkv_kernel,
        grid=(b,),
        out_shape=(jax.ShapeDtypeStruct((b, d, m), BF16), jax.ShapeDtypeStruct((b, m, d), BF16)),
        in_specs=[pl.BlockSpec((1, m, d), lambda i: (i, 0, 0)),
                  _const_spec((1, d)), _const_spec((d, d)), _const_spec((d, d))],
        out_specs=(pl.BlockSpec((1, d, m), lambda i: (i, 0, 0)),
                   pl.BlockSpec((1, m, d), lambda i: (i, 0, 0))),
        compiler_params=_params(),
        name="mem_kv",
    )(mem, g, wkT, wv)


def _inproj_kernel(x_ref, g_ref, wq_ref, wk_ref, wv_ref, wu_ref, wvs_ref, wga_ref, wgb_ref,
                   avg_ref, lng_ref, lnb_ref, sgw_ref, sgb_ref, wbsg_ref,
                   q_ref, kT_ref, v_ref, ga_ref, mb_ref, wm_scr, ysg_scr):
    pair_rows = 2 * SG_BLOCK

    @pl.when(pl.program_id(0) == 0)
    def _():
        t = lax.broadcasted_iota(jnp.int32, (pair_rows, SG_BLOCK), 0) % SG_BLOCK
        s = lax.broadcasted_iota(jnp.int32, (pair_rows, SG_BLOCK), 1)
        mask = (s // CHUNK) <= (t // CHUNK)
        for j in range(SG_GROUPS // 2):
            wm_scr[j] = jnp.where(mask, sgw_ref[j], 0.0).astype(BF16)

    def group_mean(t):
        tb = t.astype(BF16)
        w = avg_ref.shape[0]
        return jnp.concatenate([_dot(tb[:, c:c + w], avg_ref[...])
                                for c in range(0, SG_WIDTH, w)], axis=1)

    lane = lax.broadcasted_iota(jnp.int32, (SG_BLOCK, LANES), 1)

    def stages(rows):
        st = {}

        def project():
            st["hb"] = _rmsnorm(x_ref[rows, :], g_ref[...]).astype(BF16)
            st["vs"] = _dot(st["hb"], wvs_ref[...])
            st["u"] = _dot(st["hb"], wu_ref[...])
            q_ref[rows, :] = (_dot(st["hb"], wq_ref[...])
                              * (ATT_HEAD_DIM ** -0.5 * LOG2E)).astype(BF16)

        def center():
            st["vs"] = jax.nn.gelu(st["vs"])
            mean = group_mean(st["vs"])
            kT_ref[:, rows] = _dot(st["hb"], wk_ref[...]).T.astype(BF16)
            st["dev"] = st.pop("vs") - mean

        def scale():
            var = group_mean(st["dev"] * st["dev"])
            v_ref[rows, :] = _dot(st["hb"], wv_ref[...]).astype(BF16)
            st["ga"] = _dot(st["hb"], wga_ref[...])
            st["vln"] = (st.pop("dev") * lax.rsqrt(var + EPS) * lng_ref[...]
                         + lnb_ref[...]).astype(BF16)
            st["u"] = jax.nn.gelu(st["u"])

        def gate():
            for n in range(0, rows.stop - rows.start, SG_BLOCK):
                blk = slice(n, n + SG_BLOCK)
                out = slice(rows.start + n, rows.start + n + SG_BLOCK)
                for j in range(SG_GROUPS // 2):
                    cols = slice(j * LANES, (j + 1) * LANES)
                    r = _dot(wm_scr[j], st["vln"][blk, cols])
                    sv = (jnp.where(lane < SG_GROUP_DIM, r[:SG_BLOCK], r[SG_BLOCK:])
                          + sgb_ref[:, cols])
                    ysg_scr[out, cols] = (st["u"][blk, cols] * sv).astype(BF16)

        def merge():
            gate_b = jax.nn.sigmoid(_dot(st["hb"], wgb_ref[...]))
            mb_ref[rows, :] = (gate_b * _dot(ysg_scr[rows, :], wbsg_ref[...])).astype(BF16)
            ga_ref[rows, :] = jax.nn.sigmoid(st.pop("ga")).astype(BF16)

        return [project, center, scale, gate, merge]

    groups = [stages(slice(r, r + INPROJ_ROWS)) for r in range(0, x_ref.shape[0], INPROJ_ROWS)]
    for stage in zip(*groups):
        for run in stage:
            run()


def _inproj(x, g, wq, wk, wv, wu, wvs, wga, wgb, avg, lng, lnb, sgw, sgb, wbsg):
    n, d = x.shape
    tm = INPROJ_TILE
    assert tm % SG_BLOCK == 0
    tile = lambda w: pl.BlockSpec((tm, w), lambda i: (i, 0))
    return pl.pallas_call(
        _inproj_kernel,
        grid=(n // tm,),
        out_shape=(jax.ShapeDtypeStruct((n, ATT_WIDTH), BF16),
                   jax.ShapeDtypeStruct((ATT_WIDTH, n), BF16),
                   jax.ShapeDtypeStruct((n, ATT_WIDTH), BF16),
                   jax.ShapeDtypeStruct((n, d), BF16),
                   jax.ShapeDtypeStruct((n, d), BF16)),
        in_specs=[tile(d), _const_spec((1, d)),
                  _const_spec(wq.shape), _const_spec(wk.shape), _const_spec(wv.shape),
                  _const_spec(wu.shape), _const_spec(wvs.shape), _const_spec(wga.shape),
                  _const_spec(wgb.shape), _const_spec(avg.shape), _const_spec(lng.shape),
                  _const_spec(lnb.shape), _const_spec(sgw.shape), _const_spec(sgb.shape),
                  _const_spec(wbsg.shape)],
        out_specs=(tile(ATT_WIDTH), pl.BlockSpec((ATT_WIDTH, tm), lambda i: (0, i)),
                   tile(ATT_WIDTH), tile(d), tile(d)),
        scratch_shapes=[pltpu.VMEM((SG_GROUPS // 2, 2 * SG_BLOCK, SG_BLOCK), BF16),
                        pltpu.VMEM((tm, SG_WIDTH), BF16)],
        compiler_params=_params(),
        name="inproj_sgu",
    )(x, g, wq, wk, wv, wu, wvs, wga, wgb, avg, lng, lnb, sgw, sgb, wbsg)


def _attn_kernel(tiles_per_seq, x_ref, q_ref, kTp_ref, kTc_ref, vp_ref, vc_ref, ga_ref, mb_ref,
                 bias_ref, wbatt_ref, wout_ref, gx_ref, wxq_ref, kmT_ref, vm_ref, wxo_ref,
                 o_ref, kT_scr, v_scr, yatt_scr, xo_scr):
    tm = x_ref.shape[0]
    d = x_ref.shape[1]
    first_tile = (pl.program_id(0) % tiles_per_seq) == 0

    kcat = jnp.concatenate([kTp_ref[...], kTc_ref[...]], axis=1)
    low_rows = (lax.broadcasted_iota(jnp.int32, kcat.shape, 0) % LANES) < ATT_HEAD_DIM
    kT_scr[0] = jnp.where(low_rows, kcat, jnp.zeros_like(kcat))
    kT_scr[1] = jnp.where(low_rows, jnp.zeros_like(kcat), kcat)
    vcat = jnp.concatenate([vp_ref[...], vc_ref[...]], axis=0)
    low_half = (lax.broadcasted_iota(jnp.int32, vcat.shape, 1) % LANES) < ATT_HEAD_DIM
    v_scr[0] = jnp.where(low_half, vcat, jnp.ones_like(vcat))
    v_scr[1] = jnp.where(low_half, jnp.ones_like(vcat), vcat)

    lane = lax.broadcasted_iota(jnp.int32, (Q_SUB, LANES), 1)
    n_blocks = WINDOW // LANES

    def window(j, h):
        q0 = j * Q_SUB
        cols = slice((h // 2) * LANES, (h // 2 + 1) * LANES)
        return q0, q0, cols

    def scores(j, p):
        q0, w0, cols = window(j, 2 * p)
        qp = q_ref[q0:q0 + Q_SUB, cols]
        kw = jnp.concatenate([kT_scr[0, cols, w0:w0 + WINDOW], kT_scr[1, cols, w0:w0 + WINDOW]],
                             axis=1)
        s = _dot(qp, kw)
        heads = []
        for hh in range(2):
            h = 2 * p + hh
            blocks = []
            for b in range(n_blocks):
                before_start = (b + 1) * LANES <= PREV_KEYS - q0
                entry = jnp.where(first_tile, ATT_HEADS, h) if before_start else h
                blk = slice(b * LANES, (b + 1) * LANES)
                sblk = slice(hh * WINDOW + b * LANES, hh * WINDOW + (b + 1) * LANES)
                blocks.append(s[:, sblk] + bias_ref[entry, :, blk])
            heads.append(jnp.concatenate(blocks, axis=1))
        return heads

    def attend(j, h, s):
        _, w0, cols = window(j, h)
        p = jnp.exp2(s - jnp.max(s, axis=-1, keepdims=True)).astype(BF16)
        o = _dot(p, v_scr[h % 2, w0:w0 + WINDOW, cols])
        return o / pltpu.roll(o, ATT_HEAD_DIM, axis=1)

    steps = [(j, p) for j in range(tm // Q_SUB) for p in range(ATT_HEADS // 2)]
    pending = [scores(*st) for st in steps[:SCORE_LOOKAHEAD]]
    for k, (j, p) in enumerate(steps):
        if k + SCORE_LOOKAHEAD < len(steps):
            pending.append(scores(*steps[k + SCORE_LOOKAHEAD]))
        s_even, s_odd = pending.pop(0)
        o_even = attend(j, 2 * p, s_even)
        o_odd = attend(j, 2 * p + 1, s_odd)
        q0, _, cols = window(j, 2 * p)
        y = jnp.where(lane < ATT_HEAD_DIM, o_even, o_odd)
        yatt_scr[q0:q0 + Q_SUB, cols] = y.astype(BF16)

    groups = [slice(r, r + TAIL_ROWS) for r in range(0, tm, TAIL_ROWS)]
    xd = d // XATT_HEADS
    heads = [slice(h * xd, (h + 1) * xd) for h in range(XATT_HEADS)]
    att = [_dot(yatt_scr[g, :], wbatt_ref[...]) for g in groups]
    x1 = []
    for g, a in zip(groups, att):
        merged = ga_ref[g, :].astype(F32) * a + mb_ref[g, :].astype(F32)
        x1.append(x_ref[g, :] + _dot(merged.astype(BF16), wout_ref[...]))
    qx = [(_dot(_rmsnorm(v, gx_ref[...]).astype(BF16), wxq_ref[...])
           * (xd ** -0.5 * LOG2E)).astype(BF16) for v in x1]
    xs = [[_dot(q[:, cols], kmT_ref[0, cols, :]) for q in qx] for cols in heads]
    for cols, per_group in zip(heads, xs):
        for g, s in zip(groups, per_group):
            e = jnp.exp2(s - jnp.max(s, axis=-1, keepdims=True))
            den = jnp.sum(e, axis=-1, keepdims=True)
            xo_scr[g, cols] = (_dot(e.astype(BF16), vm_ref[0, :, cols]) / den).astype(BF16)
    for g, v in zip(groups, x1):
        o_ref[g, :] = v + _dot(xo_scr[g, :], wxo_ref[...])


def _attn(x, q, kT, v, ga, mb, bias, wbatt, wout, gx, wxq, kmT, vm, wxo, seq):
    n, d = x.shape
    tm = TOKEN_TILE
    tps = seq // tm
    mem_len = vm.shape[1]
    tile = lambda w: pl.BlockSpec((tm, w), lambda i: (i, 0))
    prev = lambda i: jnp.maximum(i * (tm // PREV_KEYS) - 1, 0)
    return pl.pallas_call(
        functools.partial(_attn_kernel, tps),
        grid=(n // tm,),
        out_shape=jax.ShapeDtypeStruct((n, d), F32),
        in_specs=[tile(d), tile(ATT_WIDTH),
                  pl.BlockSpec((ATT_WIDTH, PREV_KEYS), lambda i: (0, prev(i))),
                  pl.BlockSpec((ATT_WIDTH, tm), lambda i: (0, i)),
                  pl.BlockSpec((PREV_KEYS, ATT_WIDTH), lambda i: (prev(i), 0)),
                  tile(ATT_WIDTH), tile(d), tile(d),
                  _const_spec(bias.shape), _const_spec(wbatt.shape), _const_spec(wout.shape),
                  _const_spec((1, d)), _const_spec(wxq.shape),
                  pl.BlockSpec((1, d, mem_len), lambda i: (i // tps, 0, 0)),
                  pl.BlockSpec((1, mem_len, d), lambda i: (i // tps, 0, 0)),
                  _const_spec(wxo.shape)],
        out_specs=tile(d),
        scratch_shapes=[pltpu.VMEM((2, ATT_WIDTH, PREV_KEYS + tm), BF16),
                        pltpu.VMEM((2, PREV_KEYS + tm, ATT_WIDTH), BF16),
                        pltpu.VMEM((tm, ATT_WIDTH), BF16),
                        pltpu.VMEM((tm, d), BF16)],
        compiler_params=_params(),
        name="attn_merge_xattn",
    )(x, q, kT, kT, v, v, ga, mb, bias, wbatt, wout, gx, wxq, kmT, vm, wxo)


def _ffn_kernel(x_ref, g_ref, wfi_ref, wfo_ref, gfin_ref, o_ref, a_scr):
    d_ff = wfo_ref.shape[0]
    subs = [slice(r, r + FFN_SUB) for r in range(0, x_ref.shape[0], FFN_SUB)]
    hb_next = _rmsnorm(x_ref[subs[0], :], g_ref[...]).astype(BF16)
    for r, rows in enumerate(subs):
        hb = hb_next
        for c in range(d_ff // FF_CHUNK):
            gate = _dot(hb, wfi_ref[:, c * FF_CHUNK:(c + 1) * FF_CHUNK])
            up = _dot(hb, wfi_ref[:, d_ff + c * FF_CHUNK:d_ff + (c + 1) * FF_CHUNK])
            a_scr[rows, c * FF_CHUNK:(c + 1) * FF_CHUNK] = (jax.nn.silu(gate) * up).astype(BF16)
            if c == 0 and r + 1 < len(subs):
                hb_next = _rmsnorm(x_ref[subs[r + 1], :], g_ref[...]).astype(BF16)
        x3 = x_ref[rows, :] + _dot(a_scr[rows, :], wfo_ref[...])
        o_ref[rows, :] = _rmsnorm(x3, gfin_ref[...])


def _ffn(x, g, wfi, wfo, gfin):
    n, d = x.shape
    tm = FFN_TILE
    d_ff = wfo.shape[0]
    assert d_ff % FF_CHUNK == 0
    tile = pl.BlockSpec((tm, d), lambda i: (i, 0))
    return pl.pallas_call(
        _ffn_kernel,
        grid=(n // tm,),
        out_shape=jax.ShapeDtypeStruct((n, d), F32),
        in_specs=[tile, _const_spec((1, d)), _const_spec(wfi.shape), _const_spec(wfo.shape),
                  _const_spec((1, d))],
        out_specs=tile,
        scratch_shapes=[pltpu.VMEM((tm, d_ff), BF16)],
        compiler_params=_params(),
        name="ffn_final_norm",
    )(x, g, wfi, wfo, gfin)


def kernel(x, mem, norm_mix_g, w_in, rel_bias, sg_ln_g, sg_ln_b, sg_w, sg_b, w_branch_att,
           w_branch_sg, w_out, norm_xattn_g, norm_mem_g, w_xq, w_xkv, w_xo, norm_ffn_g,
           w_ffn_in, w_ffn_out, norm_final_g):
    b, s, d = x.shape
    depth = w_in.shape[0]
    assert s % TOKEN_TILE == 0 and TOKEN_TILE % PREV_KEYS == 0 and TOKEN_TILE % Q_SUB == 0
    assert (b * s) % FFN_TILE == 0 and (b * s) % INPROJ_TILE == 0
    row = lambda a: a.reshape(1, -1).astype(F32)
    grp = np.arange(MXU_WIDTH) // SG_GROUP_DIM
    avg = jnp.asarray((grp[:, None] == grp[None, :]) / SG_GROUP_DIM, BF16)

    xf = x.reshape(b * s, d)
    for l in range(depth):
        bounds = np.cumsum([0, ATT_WIDTH, ATT_WIDTH, ATT_WIDTH, SG_WIDTH, SG_WIDTH, d, d])
        wq, wk, wv, wu, wvs, wga, wgb = (
            w_in[l, :, int(lo):int(hi)].astype(BF16) for lo, hi in zip(bounds[:-1], bounds[1:]))
        sgw = sg_w[l].reshape(SG_GROUPS // 2, 2 * SG_BLOCK, SG_BLOCK)
        sgb = jnp.repeat(sg_b[l].T, SG_GROUP_DIM, axis=1)

        bias = _bias_table(rel_bias[l])
        kmT, vm = _mem_kv(mem, row(norm_mem_g[l]), w_xkv[l, :, :d].T.astype(BF16),
                          w_xkv[l, :, d:].astype(BF16))
        q, kT, v, ga, mb = _inproj(
            xf, row(norm_mix_g[l]), wq, wk, wv, wu, wvs, wga, wgb, avg,
            row(sg_ln_g[l]), row(sg_ln_b[l]), sgw, sgb, w_branch_sg[l].astype(BF16))
        xf = _attn(xf, q, kT, v, ga, mb, bias, w_branch_att[l].astype(BF16),
                   w_out[l].astype(BF16), row(norm_xattn_g[l]), w_xq[l].astype(BF16),
                   kmT, vm, w_xo[l].astype(BF16), s)
        assert depth == 1
        xf = _ffn(xf, row(norm_ffn_g[l]), w_ffn_in[l].astype(BF16), w_ffn_out[l].astype(BF16),
                  row(norm_final_g))
    return xf.reshape(b, s, d)
```

```python
import functools

import jax
import jax.numpy as jnp
import numpy as np
from jax import lax
from jax.experimental import pallas as pl
from jax.experimental.pallas import tpu as pltpu

CHUNK = 64
N_PREV_CHUNKS = 8
ATT_HEADS = 8
ATT_HEAD_DIM = 64
ATT_WIDTH = ATT_HEADS * ATT_HEAD_DIM
REL_CLIP = 128
SG_BLOCK = 128
SG_GROUPS = 8
SG_GROUP_DIM = 64
SG_WIDTH = SG_GROUPS * SG_GROUP_DIM
XATT_HEADS = 4
EPS = 1e-6
NEG_INF = -1e30
LOG2E = 1.4426950408889634

LANES = 128
MXU_WIDTH = 256
VMEM_LIMIT_BYTES = 62 * 1024 * 1024

TOKEN_TILE = 1024
TAIL_ROWS = 512
INPROJ_TILE = 1024
FFN_TILE = 1024
FFN_SUB = 512
Q_SUB = 2 * CHUNK
PREV_KEYS = N_PREV_CHUNKS * CHUNK
WINDOW = PREV_KEYS + Q_SUB
FF_CHUNK = 256
SCORE_LOOKAHEAD = 2

F32 = jnp.float32
BF16 = jnp.bfloat16
_NT = (((1,), (1,)), ((), ()))


def _dot(a, b):
    return jnp.dot(a, b, preferred_element_type=F32)


def _dot_nt(a, b):
    return lax.dot_general(a, b, _NT, preferred_element_type=F32)


def _rmsnorm(x, g):
    return x * lax.rsqrt(jnp.mean(x * x, axis=-1, keepdims=True) + EPS) * g


def _const_spec(shape):
    zeros = (0,) * len(shape)
    return pl.BlockSpec(shape, lambda *_: zeros, pipeline_mode=pl.Buffered(1))


def _params():
    return pltpu.CompilerParams(dimension_semantics=("arbitrary",),
                                vmem_limit_bytes=VMEM_LIMIT_BYTES)


def _bias_kernel(rb_ref, o_ref):
    near_cols = Q_SUB + REL_CLIP
    far_cols = WINDOW - near_cols
    width = near_cols + Q_SUB
    c = lax.broadcasted_iota(jnp.int32, (1, width), 1)
    idx = jnp.clip(near_cols - c, -REL_CLIP, REL_CLIP) + REL_CLIP
    diag = jnp.zeros((ATT_HEADS, width), F32)
    for j in range(2 * REL_CLIP + 1):
        diag = jnp.where(idx == j, rb_ref[:, j:j + 1], diag)
    diag = diag * LOG2E

    row = lax.broadcasted_iota(jnp.int32, (Q_SUB, width), 0)
    qc = lax.broadcasted_iota(jnp.int32, (Q_SUB, WINDOW), 0) // CHUNK
    kc = lax.broadcasted_iota(jnp.int32, (Q_SUB, WINDOW), 1) // CHUNK
    valid = (kc >= qc) & (kc <= qc + N_PREV_CHUNKS)
    for h in range(ATT_HEADS):
        t = jnp.broadcast_to(diag[h:h + 1, :], (Q_SUB, width))
        for bit in range(Q_SUB.bit_length() - 1):
            t = jnp.where(((row >> bit) & 1) == 1, pltpu.roll(t, 1 << bit, axis=1), t)
        far = jnp.broadcast_to(diag[h:h + 1, 0:1], (Q_SUB, far_cols))
        o_ref[h] = jnp.where(valid, jnp.concatenate([far, t[:, Q_SUB:]], axis=1), NEG_INF)
    o_ref[ATT_HEADS] = jnp.full((Q_SUB, WINDOW), NEG_INF, F32)


def _bias_table(rel_bias):
    return pl.pallas_call(
        _bias_kernel,
        out_shape=jax.ShapeDtypeStruct((ATT_HEADS + 1, Q_SUB, WINDOW), F32),
        in_specs=[pl.BlockSpec(memory_space=pltpu.VMEM)],
        out_specs=pl.BlockSpec(memory_space=pltpu.VMEM),
        name="bias_table",
    )(rel_bias)


def _memkv_kernel(mem_ref, g_ref, wkT_ref, wv_ref, kT_ref, v_ref):
    mn = _rmsnorm(mem_ref[0], g_ref[...]).astype(BF16)
    kT_ref[0] = _dot_nt(wkT_ref[...], mn).astype(BF16)
    v_ref[0] = _dot(mn, wv_ref[...]).astype(BF16)


def _mem_kv(mem, g, wkT, wv):
    b, m, d = mem.shape
    return pl.pallas_call(
        _memkv_kernel,
        grid=(b,),
        out_shape=(jax.ShapeDtypeStruct((b, d, m), BF16), jax.ShapeDtypeStruct((b, m, d), BF16)),
        in_specs=[pl.BlockSpec((1, m, d), lambda i: (i, 0, 0)),
                  _const_spec((1, d)), _const_spec((d, d)), _const_spec((d, d))],
        out_specs=(pl.BlockSpec((1, d, m), lambda i: (i, 0, 0)),
                   pl.BlockSpec((1, m, d), lambda i: (i, 0, 0))),
        compiler_params=_params(),
        name="mem_kv",
    )(mem, g, wkT, wv)


def _inproj_kernel(x_ref, g_ref, wq_ref, wk_ref, wv_ref, wu_ref, wvs_ref, wga_ref, wgb_ref,
                   avg_ref, lng_ref, lnb_ref, sgw_ref, sgb_ref, wbsg_ref,
                   q_ref, kT_ref, v_ref, ga_ref, mb_ref, wm_scr, ysg_scr):
    pair_rows = 2 * SG_BLOCK

    @pl.when(pl.program_id(0) == 0)
    def _():
        t = lax.broadcasted_iota(jnp.int32, (pair_rows, SG_BLOCK), 0) % SG_BLOCK
        s = lax.broadcasted_iota(jnp.int32, (pair_rows, SG_BLOCK), 1)
        mask = (s // CHUNK) <= (t // CHUNK)
        for j in range(SG_GROUPS // 2):
            wm_scr[j] = jnp.where(mask, sgw_ref[j], 0.0).astype(BF16)

    hb = _rmsnorm(x_ref[...], g_ref[...]).astype(BF16)
    vs_raw = _dot(hb, wvs_ref[...])
    u_raw = _dot(hb, wu_ref[...])
    q_ref[...] = (_dot(hb, wq_ref[...]) * (ATT_HEAD_DIM ** -0.5 * LOG2E)).astype(BF16)
    vs = jax.nn.gelu(vs_raw)

    def group_mean(t):
        tb = t.astype(BF16)
        w = avg_ref.shape[0]
        return jnp.concatenate([_dot(tb[:, c:c + w], avg_ref[...])
                                for c in range(0, SG_WIDTH, w)], axis=1)

    mean = group_mean(vs)
    kT_ref[...] = _dot(hb, wk_ref[...]).T.astype(BF16)
    dev = vs - mean
    var = group_mean(dev * dev)
    v_ref[...] = _dot(hb, wv_ref[...]).astype(BF16)
    ga_raw = _dot(hb, wga_ref[...])
    vln = (dev * lax.rsqrt(var + EPS) * lng_ref[...] + lnb_ref[...]).astype(BF16)
    u = jax.nn.gelu(u_raw)

    lane = lax.broadcasted_iota(jnp.int32, (SG_BLOCK, LANES), 1)
    for n in range(x_ref.shape[0] // SG_BLOCK):
        rows = slice(n * SG_BLOCK, (n + 1) * SG_BLOCK)
        for j in range(SG_GROUPS // 2):
            cols = slice(j * LANES, (j + 1) * LANES)
            r = _dot(wm_scr[j], vln[rows, cols])
            sv = jnp.where(lane < SG_GROUP_DIM, r[:SG_BLOCK], r[SG_BLOCK:]) + sgb_ref[:, cols]
            ysg_scr[rows, cols] = (u[rows, cols] * sv).astype(BF16)

    gate_b = jax.nn.sigmoid(_dot(hb, wgb_ref[...]))
    mb_ref[...] = (gate_b * _dot(ysg_scr[...], wbsg_ref[...])).astype(BF16)
    ga_ref[...] = jax.nn.sigmoid(ga_raw).astype(BF16)


def _inproj(x, g, wq, wk, wv, wu, wvs, wga, wgb, avg, lng, lnb, sgw, sgb, wbsg):
    n, d = x.shape
    tm = INPROJ_TILE
    assert tm % SG_BLOCK == 0
    tile = lambda w: pl.BlockSpec((tm, w), lambda i: (i, 0))
    return pl.pallas_call(
        _inproj_kernel,
        grid=(n // tm,),
        out_shape=(jax.ShapeDtypeStruct((n, ATT_WIDTH), BF16),
                   jax.ShapeDtypeStruct((ATT_WIDTH, n), BF16),
                   jax.ShapeDtypeStruct((n, ATT_WIDTH), BF16),
                   jax.ShapeDtypeStruct((n, d), BF16),
                   jax.ShapeDtypeStruct((n, d), BF16)),
        in_specs=[tile(d), _const_spec((1, d)),
                  _const_spec(wq.shape), _const_spec(wk.shape), _const_spec(wv.shape),
                  _const_spec(wu.shape), _const_spec(wvs.shape), _const_spec(wga.shape),
                  _const_spec(wgb.shape), _const_spec(avg.shape), _const_spec(lng.shape),
                  _const_spec(lnb.shape), _const_spec(sgw.shape), _const_spec(sgb.shape),
                  _const_spec(wbsg.shape)],
        out_specs=(tile(ATT_WIDTH), pl.BlockSpec((ATT_WIDTH, tm), lambda i: (0, i)),
                   tile(ATT_WIDTH), tile(d), tile(d)),
        scratch_shapes=[pltpu.VMEM((SG_GROUPS // 2, 2 * SG_BLOCK, SG_BLOCK), BF16),
                        pltpu.VMEM((tm, SG_WIDTH), BF16)],
        compiler_params=_params(),
        name="inproj_sgu",
    )(x, g, wq, wk, wv, wu, wvs, wga, wgb, avg, lng, lnb, sgw, sgb, wbsg)


def _attn_kernel(tiles_per_seq, x_ref, q_ref, kTp_ref, kTc_ref, vp_ref, vc_ref, ga_ref, mb_ref,
                 bias_ref, wbatt_ref, wout_ref, gx_ref, wxq_ref, kmT_ref, vm_ref, wxo_ref,
                 o_ref, kT_scr, v_scr, yatt_scr, xo_scr):
    tm = x_ref.shape[0]
    d = x_ref.shape[1]
    first_tile = (pl.program_id(0) % tiles_per_seq) == 0

    kcat = jnp.concatenate([kTp_ref[...], kTc_ref[...]], axis=1)
    low_rows = (lax.broadcasted_iota(jnp.int32, kcat.shape, 0) % LANES) < ATT_HEAD_DIM
    kT_scr[0] = jnp.where(low_rows, kcat, jnp.zeros_like(kcat))
    kT_scr[1] = jnp.where(low_rows, jnp.zeros_like(kcat), kcat)
    vcat = jnp.concatenate([vp_ref[...], vc_ref[...]], axis=0)
    low_half = (lax.broadcasted_iota(jnp.int32, vcat.shape, 1) % LANES) < ATT_HEAD_DIM
    v_scr[0] = jnp.where(low_half, vcat, jnp.ones_like(vcat))
    v_scr[1] = jnp.where(low_half, jnp.ones_like(vcat), vcat)

    lane = lax.broadcasted_iota(jnp.int32, (Q_SUB, LANES), 1)
    n_blocks = WINDOW // LANES

    def window(j, h):
        q0 = j * Q_SUB
        cols = slice((h // 2) * LANES, (h // 2 + 1) * LANES)
        return q0, q0, cols

    def scores(j, p):
        q0, w0, cols = window(j, 2 * p)
        qp = q_ref[q0:q0 + Q_SUB, cols]
        kw = jnp.concatenate([kT_scr[0, cols, w0:w0 + WINDOW], kT_scr[1, cols, w0:w0 + WINDOW]],
                             axis=1)
        s = _dot(qp, kw)
        heads = []
        for hh in range(2):
            h = 2 * p + hh
            blocks = []
            for b in range(n_blocks):
                before_start = (b + 1) * LANES <= PREV_KEYS - q0
                entry = jnp.where(first_tile, ATT_HEADS, h) if before_start else h
                blk = slice(b * LANES, (b + 1) * LANES)
                sblk = slice(hh * WINDOW + b * LANES, hh * WINDOW + (b + 1) * LANES)
                blocks.append(s[:, sblk] + bias_ref[entry, :, blk])
            heads.append(jnp.concatenate(blocks, axis=1))
        return heads

    def attend(j, h, s):
        _, w0, cols = window(j, h)
        p = jnp.exp2(s - jnp.max(s, axis=-1, keepdims=True)).astype(BF16)
        o = _dot(p, v_scr[h % 2, w0:w0 + WINDOW, cols])
        return o / pltpu.roll(o, ATT_HEAD_DIM, axis=1)

    steps = [(j, p) for j in range(tm // Q_SUB) for p in range(ATT_HEADS // 2)]
    pending = [scores(*st) for st in steps[:SCORE_LOOKAHEAD]]
    for k, (j, p) in enumerate(steps):
        if k + SCORE_LOOKAHEAD < len(steps):
            pending.append(scores(*steps[k + SCORE_LOOKAHEAD]))
        s_even, s_odd = pending.pop(0)
        o_even = attend(j, 2 * p, s_even)
        o_odd = attend(j, 2 * p + 1, s_odd)
        q0, _, cols = window(j, 2 * p)
        y = jnp.where(lane < ATT_HEAD_DIM, o_even, o_odd)
        yatt_scr[q0:q0 + Q_SUB, cols] = y.astype(BF16)

    groups = [slice(r, r + TAIL_ROWS) for r in range(0, tm, TAIL_ROWS)]
    xd = d // XATT_HEADS
    heads = [slice(h * xd, (h + 1) * xd) for h in range(XATT_HEADS)]
    att = [_dot(yatt_scr[g, :], wbatt_ref[...]) for g in groups]
    x1 = []
    for g, a in zip(groups, att):
        merged = ga_ref[g, :].astype(F32) * a + mb_ref[g, :].astype(F32)
        x1.append(x_ref[g, :] + _dot(merged.astype(BF16), wout_ref[...]))
    qx = [(_dot(_rmsnorm(v, gx_ref[...]).astype(BF16), wxq_ref[...])
           * (xd ** -0.5 * LOG2E)).astype(BF16) for v in x1]
    xs = [[_dot(q[:, cols], kmT_ref[0, cols, :]) for q in qx] for cols in heads]
    for cols, per_group in zip(heads, xs):
        for g, s in zip(groups, per_group):
            e = jnp.exp2(s - jnp.max(s, axis=-1, keepdims=True))
            den = jnp.sum(e, axis=-1, keepdims=True)
            xo_scr[g, cols] = (_dot(e.astype(BF16), vm_ref[0, :, cols]) / den).astype(BF16)
    for g, v in zip(groups, x1):
        o_ref[g, :] = v + _dot(xo_scr[g, :], wxo_ref[...])


def _attn(x, q, kT, v, ga, mb, bias, wbatt, wout, gx, wxq, kmT, vm, wxo, seq):
    n, d = x.shape
    tm = TOKEN_TILE
    tps = seq // tm
    mem_len = vm.shape[1]
    tile = lambda w: pl.BlockSpec((tm, w), lambda i: (i, 0))
    prev = lambda i: jnp.maximum(i * (tm // PREV_KEYS) - 1, 0)
    return pl.pallas_call(
        functools.partial(_attn_kernel, tps),
        grid=(n // tm,),
        out_shape=jax.ShapeDtypeStruct((n, d), F32),
        in_specs=[tile(d), tile(ATT_WIDTH),
                  pl.BlockSpec((ATT_WIDTH, PREV_KEYS), lambda i: (0, prev(i))),
                  pl.BlockSpec((ATT_WIDTH, tm), lambda i: (0, i)),
                  pl.BlockSpec((PREV_KEYS, ATT_WIDTH), lambda i: (prev(i), 0)),
                  tile(ATT_WIDTH), tile(d), tile(d),
                  _const_spec(bias.shape), _const_spec(wbatt.shape), _const_spec(wout.shape),
                  _const_spec((1, d)), _const_spec(wxq.shape),
                  pl.BlockSpec((1, d, mem_len), lambda i: (i // tps, 0, 0)),
                  pl.BlockSpec((1, mem_len, d), lambda i: (i // tps, 0, 0)),
                  _const_spec(wxo.shape)],
        out_specs=tile(d),
        scratch_shapes=[pltpu.VMEM((2, ATT_WIDTH, PREV_KEYS + tm), BF16),
                        pltpu.VMEM((2, PREV_KEYS + tm, ATT_WIDTH), BF16),
                        pltpu.VMEM((tm, ATT_WIDTH), BF16),
                        pltpu.VMEM((tm, d), BF16)],
        compiler_params=_params(),
        name="attn_merge_xattn",
    )(x, q, kT, kT, v, v, ga, mb, bias, wbatt, wout, gx, wxq, kmT, vm, wxo)


def _ffn_kernel(x_ref, g_ref, wfi_ref, wfo_ref, gfin_ref, o_ref, a_scr):
    d_ff = wfo_ref.shape[0]
    subs = [slice(r, r + FFN_SUB) for r in range(0, x_ref.shape[0], FFN_SUB)]
    def finish(rows):
        x3 = x_ref[rows, :] + _dot(a_scr[rows, :], wfo_ref[...])
        o_ref[rows, :] = _rmsnorm(x3, gfin_ref[...])

    hb_next = _rmsnorm(x_ref[subs[0], :], g_ref[...]).astype(BF16)
    for r, rows in enumerate(subs):
        hb = hb_next
        for c in range(d_ff // FF_CHUNK):
            gate = _dot(hb, wfi_ref[:, c * FF_CHUNK:(c + 1) * FF_CHUNK])
            up = _dot(hb, wfi_ref[:, d_ff + c * FF_CHUNK:d_ff + (c + 1) * FF_CHUNK])
            a_scr[rows, c * FF_CHUNK:(c + 1) * FF_CHUNK] = (jax.nn.silu(gate) * up).astype(BF16)
            if c == 0 and r + 1 < len(subs):
                hb_next = _rmsnorm(x_ref[subs[r + 1], :], g_ref[...]).astype(BF16)
            if c == 0 and r > 0:
                finish(subs[r - 1])
    finish(subs[-1])


def _ffn(x, g, wfi, wfo, gfin):
    n, d = x.shape
    tm = FFN_TILE
    d_ff = wfo.shape[0]
    assert d_ff % FF_CHUNK == 0
    tile = pl.BlockSpec((tm, d), lambda i: (i, 0))
    return pl.pallas_call(
        _ffn_kernel,
        grid=(n // tm,),
        out_shape=jax.ShapeDtypeStruct((n, d), F32),
        in_specs=[tile, _const_spec((1, d)), _const_spec(wfi.shape), _const_spec(wfo.shape),
                  _const_spec((1, d))],
        out_specs=tile,
        scratch_shapes=[pltpu.VMEM((tm, d_ff), BF16)],
        compiler_params=_params(),
        name="ffn_final_norm",
    )(x, g, wfi, wfo, gfin)


def kernel(x, mem, norm_mix_g, w_in, rel_bias, sg_ln_g, sg_ln_b, sg_w, sg_b, w_branch_att,
           w_branch_sg, w_out, norm_xattn_g, norm_mem_g, w_xq, w_xkv, w_xo, norm_ffn_g,
           w_ffn_in, w_ffn_out, norm_final_g):
    b, s, d = x.shape
    depth = w_in.shape[0]
    assert s % TOKEN_TILE == 0 and TOKEN_TILE % PREV_KEYS == 0 and TOKEN_TILE % Q_SUB == 0
    assert (b * s) % FFN_TILE == 0 and (b * s) % INPROJ_TILE == 0
    row = lambda a: a.reshape(1, -1).astype(F32)
    grp = np.arange(MXU_WIDTH) // SG_GROUP_DIM
    avg = jnp.asarray((grp[:, None] == grp[None, :]) / SG_GROUP_DIM, BF16)

    xf = x.reshape(b * s, d)
    for l in range(depth):
        bounds = np.cumsum([0, ATT_WIDTH, ATT_WIDTH, ATT_WIDTH, SG_WIDTH, SG_WIDTH, d, d])
        wq, wk, wv, wu, wvs, wga, wgb = (
            w_in[l, :, int(lo):int(hi)].astype(BF16) for lo, hi in zip(bounds[:-1], bounds[1:]))
        sgw = sg_w[l].reshape(SG_GROUPS // 2, 2 * SG_BLOCK, SG_BLOCK)
        sgb = jnp.repeat(sg_b[l].T, SG_GROUP_DIM, axis=1)

        bias = _bias_table(rel_bias[l])
        kmT, vm = _mem_kv(mem, row(norm_mem_g[l]), w_xkv[l, :, :d].T.astype(BF16),
                          w_xkv[l, :, d:].astype(BF16))
        q, kT, v, ga, mb = _inproj(
            xf, row(norm_mix_g[l]), wq, wk, wv, wu, wvs, wga, wgb, avg,
            row(sg_ln_g[l]), row(sg_ln_b[l]), sgw, sgb, w_branch_sg[l].astype(BF16))
        xf = _attn(xf, q, kT, v, ga, mb, bias, w_branch_att[l].astype(BF16),
                   w_out[l].astype(BF16), row(norm_xattn_g[l]), w_xq[l].astype(BF16),
                   kmT, vm, w_xo[l].astype(BF16), s)
        assert depth == 1
        xf = _ffn(xf, row(norm_ffn_g[l]), w_ffn_in[l].astype(BF16), w_ffn_out[l].astype(BF16),
                  row(norm_final_g))
    return xf.reshape(b, s, d)
```

```python
import functools

import jax
import jax.numpy as jnp
import numpy as np
from jax import lax
from jax.experimental import pallas as pl
from jax.experimental.pallas import tpu as pltpu

CHUNK = 64
N_PREV_CHUNKS = 8
ATT_HEADS = 8
ATT_HEAD_DIM = 64
ATT_WIDTH = ATT_HEADS * ATT_HEAD_DIM
REL_CLIP = 128
SG_BLOCK = 128
SG_GROUPS = 8
SG_GROUP_DIM = 64
SG_WIDTH = SG_GROUPS * SG_GROUP_DIM
XATT_HEADS = 4
EPS = 1e-6
NEG_INF = -1e30
LOG2E = 1.4426950408889634

LANES = 128
MXU_WIDTH = 256
VMEM_LIMIT_BYTES = 62 * 1024 * 1024

TOKEN_TILE = 1024
TAIL_ROWS = 512
INPROJ_TILE = 1024
FFN_TILE = 1024
FFN_SUB = 512
Q_SUB = 2 * CHUNK
PREV_KEYS = N_PREV_CHUNKS * CHUNK
WINDOW = PREV_KEYS + Q_SUB
FF_CHUNK = 256
SCORE_LOOKAHEAD = 2

F32 = jnp.float32
BF16 = jnp.bfloat16
_NT = (((1,), (1,)), ((), ()))


def _dot(a, b):
    return jnp.dot(a, b, preferred_element_type=F32)


def _dot_nt(a, b):
    return lax.dot_general(a, b, _NT, preferred_element_type=F32)


def _rmsnorm(x, g):
    return x * lax.rsqrt(jnp.mean(x * x, axis=-1, keepdims=True) + EPS) * g


def _const_spec(shape):
    zeros = (0,) * len(shape)
    return pl.BlockSpec(shape, lambda *_: zeros, pipeline_mode=pl.Buffered(1))


def _params():
    return pltpu.CompilerParams(dimension_semantics=("arbitrary",),
                                vmem_limit_bytes=VMEM_LIMIT_BYTES)


def _bias_kernel(rb_ref, o_ref):
    near_cols = Q_SUB + REL_CLIP
    far_cols = WINDOW - near_cols
    width = near_cols + Q_SUB
    c = lax.broadcasted_iota(jnp.int32, (1, width), 1)
    idx = jnp.clip(near_cols - c, -REL_CLIP, REL_CLIP) + REL_CLIP
    diag = jnp.zeros((ATT_HEADS, width), F32)
    for j in range(2 * REL_CLIP + 1):
        diag = jnp.where(idx == j, rb_ref[:, j:j + 1], diag)
    diag = diag * LOG2E

    row = lax.broadcasted_iota(jnp.int32, (Q_SUB, width), 0)
    qc = lax.broadcasted_iota(jnp.int32, (Q_SUB, WINDOW), 0) // CHUNK
    kc = lax.broadcasted_iota(jnp.int32, (Q_SUB, WINDOW), 1) // CHUNK
    valid = (kc >= qc) & (kc <= qc + N_PREV_CHUNKS)
    for h in range(ATT_HEADS):
        t = jnp.broadcast_to(diag[h:h + 1, :], (Q_SUB, width))
        for bit in range(Q_SUB.bit_length() - 1):
            t = jnp.where(((row >> bit) & 1) == 1, pltpu.roll(t, 1 << bit, axis=1), t)
        far = jnp.broadcast_to(diag[h:h + 1, 0:1], (Q_SUB, far_cols))
        o_ref[h] = jnp.where(valid, jnp.concatenate([far, t[:, Q_SUB:]], axis=1), NEG_INF)
    o_ref[ATT_HEADS] = jnp.full((Q_SUB, WINDOW), NEG_INF, F32)


def _bias_table(rel_bias):
    return pl.pallas_call(
        _bias_kernel,
        out_shape=jax.ShapeDtypeStruct((ATT_HEADS + 1, Q_SUB, WINDOW), F32),
        in_specs=[pl.BlockSpec(memory_space=pltpu.VMEM)],
        out_specs=pl.BlockSpec(memory_space=pltpu.VMEM),
        name="bias_table",
    )(rel_bias)


def _memkv_kernel(mem_ref, g_ref, wkT_ref, wv_ref, kT_ref, v_ref):
    mn = _rmsnorm(mem_ref[0], g_ref[...]).astype(BF16)
    kT_ref[0] = _dot_nt(wkT_ref[...], mn).astype(BF16)
    v_ref[0] = _dot(mn, wv_ref[...]).astype(BF16)


def _mem_kv(mem, g, wkT, wv):
    b, m, d = mem.shape
    return pl.pallas_call(
        _memkv_kernel,
        grid=(b,),
        out_shape=(jax.ShapeDtypeStruct((b, d, m), BF16), jax.ShapeDtypeStruct((b, m, d), BF16)),
        in_specs=[pl.BlockSpec((1, m, d), lambda i: (i, 0, 0)),
                  _const_spec((1, d)), _const_spec((d, d)), _const_spec((d, d))],
        out_specs=(pl.BlockSpec((1, d, m), lambda i: (i, 0, 0)),
                   pl.BlockSpec((1, m, d), lambda i: (i, 0, 0))),
        compiler_params=_params(),
        name="mem_kv",
    )(mem, g, wkT, wv)


def _inproj_kernel(x_ref, g_ref, wq_ref, wk_ref, wv_ref, wu_ref, wvs_ref, wga_ref, wgb_ref,
                   avg_ref, lng_ref, lnb_ref, sgw_ref, sgb_ref, wbsg_ref,
                   q_ref, kT_ref, v_ref, ga_ref, mb_ref, wm_scr, ysg_scr):
    pair_rows = 2 * SG_BLOCK

    @pl.when(pl.program_id(0) == 0)
    def _():
        t = lax.broadcasted_iota(jnp.int32, (pair_rows, SG_BLOCK), 0) % SG_BLOCK
        s = lax.broadcasted_iota(jnp.int32, (pair_rows, SG_BLOCK), 1)
        mask = (s // CHUNK) <= (t // CHUNK)
        for j in range(SG_GROUPS // 2):
            wm_scr[j] = jnp.where(mask, sgw_ref[j], 0.0).astype(BF16)

    hb = _rmsnorm(x_ref[...], g_ref[...]).astype(BF16)
    vs_raw = _dot(hb, wvs_ref[...])
    u_raw = _dot(hb, wu_ref[...])
    q_ref[...] = (_dot(hb, wq_ref[...]) * (ATT_HEAD_DIM ** -0.5 * LOG2E)).astype(BF16)
    vs = jax.nn.gelu(vs_raw)

    def group_mean(t):
        tb = t.astype(BF16)
        w = avg_ref.shape[0]
        return jnp.concatenate([_dot(tb[:, c:c + w], avg_ref[...])
                                for c in range(0, SG_WIDTH, w)], axis=1)

    mean = group_mean(vs)
    kT_ref[...] = _dot(hb, wk_ref[...]).T.astype(BF16)
    dev = vs - mean
    var = group_mean(dev * dev)
    v_ref[...] = _dot(hb, wv_ref[...]).astype(BF16)
    ga_raw = _dot(hb, wga_ref[...])
    vln = (dev * lax.rsqrt(var + EPS) * lng_ref[...] + lnb_ref[...]).astype(BF16)
    u = jax.nn.gelu(u_raw)

    lane = lax.broadcasted_iota(jnp.int32, (SG_BLOCK, LANES), 1)
    for n in range(x_ref.shape[0] // SG_BLOCK):
        rows = slice(n * SG_BLOCK, (n + 1) * SG_BLOCK)
        for j in range(SG_GROUPS // 2):
            cols = slice(j * LANES, (j + 1) * LANES)
            r = _dot(wm_scr[j], vln[rows, cols])
            sv = jnp.where(lane < SG_GROUP_DIM, r[:SG_BLOCK], r[SG_BLOCK:]) + sgb_ref[:, cols]
            ysg_scr[rows, cols] = (u[rows, cols] * sv).astype(BF16)

    gate_b = jax.nn.sigmoid(_dot(hb, wgb_ref[...]))
    mb_ref[...] = (gate_b * _dot(ysg_scr[...], wbsg_ref[...])).astype(BF16)
    ga_ref[...] = jax.nn.sigmoid(ga_raw).astype(BF16)


def _inproj(x, g, wq, wk, wv, wu, wvs, wga, wgb, avg, lng, lnb, sgw, sgb, wbsg):
    n, d = x.shape
    tm = INPROJ_TILE
    assert tm % SG_BLOCK == 0
    tile = lambda w: pl.BlockSpec((tm, w), lambda i: (i, 0))
    return pl.pallas_call(
        _inproj_kernel,
        grid=(n // tm,),
        out_shape=(jax.ShapeDtypeStruct((n, ATT_WIDTH), BF16),
                   jax.ShapeDtypeStruct((ATT_WIDTH, n), BF16),
                   jax.ShapeDtypeStruct((n, ATT_WIDTH), BF16),
                   jax.ShapeDtypeStruct((n, d), BF16),
                   jax.ShapeDtypeStruct((n, d), BF16)),
        in_specs=[tile(d), _const_spec((1, d)),
                  _const_spec(wq.shape), _const_spec(wk.shape), _const_spec(wv.shape),
                  _const_spec(wu.shape), _const_spec(wvs.shape), _const_spec(wga.shape),
                  _const_spec(wgb.shape), _const_spec(avg.shape), _const_spec(lng.shape),
                  _const_spec(lnb.shape), _const_spec(sgw.shape), _const_spec(sgb.shape),
                  _const_spec(wbsg.shape)],
        out_specs=(tile(ATT_WIDTH), pl.BlockSpec((ATT_WIDTH, tm), lambda i: (0, i)),
                   tile(ATT_WIDTH), tile(d), tile(d)),
        scratch_shapes=[pltpu.VMEM((SG_GROUPS // 2, 2 * SG_BLOCK, SG_BLOCK), BF16),
                        pltpu.VMEM((tm, SG_WIDTH), BF16)],
        compiler_params=_params(),
        name="inproj_sgu",
    )(x, g, wq, wk, wv, wu, wvs, wga, wgb, avg, lng, lnb, sgw, sgb, wbsg)


def _attn_kernel(tiles_per_seq, x_ref, q_ref, kTp_ref, kTc_ref, vp_ref, vc_ref, ga_ref, mb_ref,
                 bias_ref, wbatt_ref, wout_ref, gx_ref, wxq_ref, kmT_ref, vm_ref, wxo_ref,
                 o_ref, kT_scr, v_scr, yatt_scr, xo_scr):
    tm = x_ref.shape[0]
    d = x_ref.shape[1]
    first_tile = (pl.program_id(0) % tiles_per_seq) == 0

    kcat = jnp.concatenate([kTp_ref[...], kTc_ref[...]], axis=1)
    low_rows = (lax.broadcasted_iota(jnp.int32, kcat.shape, 0) % LANES) < ATT_HEAD_DIM
    kT_scr[0] = jnp.where(low_rows, kcat, jnp.zeros_like(kcat))
    kT_scr[1] = jnp.where(low_rows, jnp.zeros_like(kcat), kcat)
    vcat = jnp.concatenate([vp_ref[...], vc_ref[...]], axis=0)
    low_half = (lax.broadcasted_iota(jnp.int32, vcat.shape, 1) % LANES) < ATT_HEAD_DIM
    v_scr[0] = jnp.where(low_half, vcat, jnp.ones_like(vcat))
    v_scr[1] = jnp.where(low_half, jnp.ones_like(vcat), vcat)

    lane = lax.broadcasted_iota(jnp.int32, (Q_SUB, LANES), 1)
    n_blocks = WINDOW // LANES

    def window(j, h):
        q0 = j * Q_SUB
        cols = slice((h // 2) * LANES, (h // 2 + 1) * LANES)
        return q0, q0, cols

    def scores(j, p):
        q0, w0, cols = window(j, 2 * p)
        qp = q_ref[q0:q0 + Q_SUB, cols]
        kw = jnp.concatenate([kT_scr[0, cols, w0:w0 + WINDOW], kT_scr[1, cols, w0:w0 + WINDOW]],
                             axis=1)
        s = _dot(qp, kw)
        heads = []
        for hh in range(2):
            h = 2 * p + hh
            blocks = []
            for b in range(n_blocks):
                before_start = (b + 1) * LANES <= PREV_KEYS - q0
                entry = jnp.where(first_tile, ATT_HEADS, h) if before_start else h
                blk = slice(b * LANES, (b + 1) * LANES)
                sblk = slice(hh * WINDOW + b * LANES, hh * WINDOW + (b + 1) * LANES)
                blocks.append(s[:, sblk] + bias_ref[entry, :, blk])
            heads.append(jnp.concatenate(blocks, axis=1))
        return heads

    def attend(j, h, s):
        _, w0, cols = window(j, h)
        p = jnp.exp2(s - jnp.max(s, axis=-1, keepdims=True)).astype(BF16)
        o = _dot(p, v_scr[h % 2, w0:w0 + WINDOW, cols])
        return o / pltpu.roll(o, ATT_HEAD_DIM, axis=1)

    steps = [(j, p) for j in range(tm // Q_SUB) for p in range(ATT_HEADS // 2)]
    pending = [scores(*st) for st in steps[:SCORE_LOOKAHEAD]]
    for k, (j, p) in enumerate(steps):
        if k + SCORE_LOOKAHEAD < len(steps):
            pending.append(scores(*steps[k + SCORE_LOOKAHEAD]))
        s_even, s_odd = pending.pop(0)
        o_even = attend(j, 2 * p, s_even)
        o_odd = attend(j, 2 * p + 1, s_odd)
        q0, _, cols = window(j, 2 * p)
        y = jnp.where(lane < ATT_HEAD_DIM, o_even, o_odd)
        yatt_scr[q0:q0 + Q_SUB, cols] = y.astype(BF16)

    groups = [slice(r, r + TAIL_ROWS) for r in range(0, tm, TAIL_ROWS)]
    xd = d // XATT_HEADS
    heads = [slice(h * xd, (h + 1) * xd) for h in range(XATT_HEADS)]
    att = [_dot(yatt_scr[g, :], wbatt_ref[...]) for g in groups]
    x1 = []
    for g, a in zip(groups, att):
        merged = ga_ref[g, :].astype(F32) * a + mb_ref[g, :].astype(F32)
        x1.append(x_ref[g, :] + _dot(merged.astype(BF16), wout_ref[...]))
    qx = [(_dot(_rmsnorm(v, gx_ref[...]).astype(BF16), wxq_ref[...])
           * (xd ** -0.5 * LOG2E)).astype(BF16) for v in x1]
    xs = [[_dot(q[:, cols], kmT_ref[0, cols, :]) for q in qx] for cols in heads]
    for gi, (g, v) in enumerate(zip(groups, x1)):
        for cols, per_group in zip(heads, xs):
            s = per_group[gi]
            e = jnp.exp2(s - jnp.max(s, axis=-1, keepdims=True))
            den = jnp.sum(e, axis=-1, keepdims=True)
            xo_scr[g, cols] = (_dot(e.astype(BF16), vm_ref[0, :, cols]) / den).astype(BF16)
        o_ref[g, :] = v + _dot(xo_scr[g, :], wxo_ref[...])


def _attn(x, q, kT, v, ga, mb, bias, wbatt, wout, gx, wxq, kmT, vm, wxo, seq):
    n, d = x.shape
    tm = TOKEN_TILE
    tps = seq // tm
    mem_len = vm.shape[1]
    tile = lambda w: pl.BlockSpec((tm, w), lambda i: (i, 0))
    prev = lambda i: jnp.maximum(i * (tm // PREV_KEYS) - 1, 0)
    return pl.pallas_call(
        functools.partial(_attn_kernel, tps),
        grid=(n // tm,),
        out_shape=jax.ShapeDtypeStruct((n, d), F32),
        in_specs=[tile(d), tile(ATT_WIDTH),
                  pl.BlockSpec((ATT_WIDTH, PREV_KEYS), lambda i: (0, prev(i))),
                  pl.BlockSpec((ATT_WIDTH, tm), lambda i: (0, i)),
                  pl.BlockSpec((PREV_KEYS, ATT_WIDTH), lambda i: (prev(i), 0)),
                  tile(ATT_WIDTH), tile(d), tile(d),
                  _const_spec(bias.shape), _const_spec(wbatt.shape), _const_spec(wout.shape),
                  _const_spec((1, d)), _const_spec(wxq.shape),
                  pl.BlockSpec((1, d, mem_len), lambda i: (i // tps, 0, 0)),
                  pl.BlockSpec((1, mem_len, d), lambda i: (i // tps, 0, 0)),
                  _const_spec(wxo.shape)],
        out_specs=tile(d),
        scratch_shapes=[pltpu.VMEM((2, ATT_WIDTH, PREV_KEYS + tm), BF16),
                        pltpu.VMEM((2, PREV_KEYS + tm, ATT_WIDTH), BF16),
                        pltpu.VMEM((tm, ATT_WIDTH), BF16),
                        pltpu.VMEM((tm, d), BF16)],
        compiler_params=_params(),
        name="attn_merge_xattn",
    )(x, q, kT, kT, v, v, ga, mb, bias, wbatt, wout, gx, wxq, kmT, vm, wxo)


def _ffn_kernel(x_ref, g_ref, wfi_ref, wfo_ref, gfin_ref, o_ref, a_scr):
    d_ff = wfo_ref.shape[0]
    subs = [slice(r, r + FFN_SUB) for r in range(0, x_ref.shape[0], FFN_SUB)]
    def finish(rows):
        x3 = x_ref[rows, :] + _dot(a_scr[rows, :], wfo_ref[...])
        o_ref[rows, :] = _rmsnorm(x3, gfin_ref[...])

    hb_next = _rmsnorm(x_ref[subs[0], :], g_ref[...]).astype(BF16)
    for r, rows in enumerate(subs):
        hb = hb_next
        for c in range(d_ff // FF_CHUNK):
            gate = _dot(hb, wfi_ref[:, c * FF_CHUNK:(c + 1) * FF_CHUNK])
            up = _dot(hb, wfi_ref[:, d_ff + c * FF_CHUNK:d_ff + (c + 1) * FF_CHUNK])
            a_scr[rows, c * FF_CHUNK:(c + 1) * FF_CHUNK] = (jax.nn.silu(gate) * up).astype(BF16)
            if c == 0 and r + 1 < len(subs):
                hb_next = _rmsnorm(x_ref[subs[r + 1], :], g_ref[...]).astype(BF16)
            if c == 0 and r > 0:
                finish(subs[r - 1])
    finish(subs[-1])


def _ffn(x, g, wfi, wfo, gfin):
    n, d = x.shape
    tm = FFN_TILE
    d_ff = wfo.shape[0]
    assert d_ff % FF_CHUNK == 0
    tile = pl.BlockSpec((tm, d), lambda i: (i, 0))
    return pl.pallas_call(
        _ffn_kernel,
        grid=(n // tm,),
        out_shape=jax.ShapeDtypeStruct((n, d), F32),
        in_specs=[tile, _const_spec((1, d)), _const_spec(wfi.shape), _const_spec(wfo.shape),
                  _const_spec((1, d))],
        out_specs=tile,
        scratch_shapes=[pltpu.VMEM((tm, d_ff), BF16)],
        compiler_params=_params(),
        name="ffn_final_norm",
    )(x, g, wfi, wfo, gfin)


def kernel(x, mem, norm_mix_g, w_in, rel_bias, sg_ln_g, sg_ln_b, sg_w, sg_b, w_branch_att,
           w_branch_sg, w_out, norm_xattn_g, norm_mem_g, w_xq, w_xkv, w_xo, norm_ffn_g,
           w_ffn_in, w_ffn_out, norm_final_g):
    b, s, d = x.shape
    depth = w_in.shape[0]
    assert s % TOKEN_TILE == 0 and TOKEN_TILE % PREV_KEYS == 0 and TOKEN_TILE % Q_SUB == 0
    assert (b * s) % FFN_TILE == 0 and (b * s) % INPROJ_TILE == 0
    row = lambda a: a.reshape(1, -1).astype(F32)
    grp = np.arange(MXU_WIDTH) // SG_GROUP_DIM
    avg = jnp.asarray((grp[:, None] == grp[None, :]) / SG_GROUP_DIM, BF16)

    xf = x.reshape(b * s, d)
    for l in range(depth):
        bounds = np.cumsum([0, ATT_WIDTH, ATT_WIDTH, ATT_WIDTH, SG_WIDTH, SG_WIDTH, d, d])
        wq, wk, wv, wu, wvs, wga, wgb = (
            w_in[l, :, int(lo):int(hi)].astype(BF16) for lo, hi in zip(bounds[:-1], bounds[1:]))
        sgw = sg_w[l].reshape(SG_GROUPS // 2, 2 * SG_BLOCK, SG_BLOCK)
        sgb = jnp.repeat(sg_b[l].T, SG_GROUP_DIM, axis=1)

        bias = _bias_table(rel_bias[l])
        kmT, vm = _mem_kv(mem, row(norm_mem_g[l]), w_xkv[l, :, :d].T.astype(BF16),
                          w_xkv[l, :, d:].astype(BF16))
        q, kT, v, ga, mb = _inproj(
            xf, row(norm_mix_g[l]), wq, wk, wv, wu, wvs, wga, wgb, avg,
            row(sg_ln_g[l]), row(sg_ln_b[l]), sgw, sgb, w_branch_sg[l].astype(BF16))
        xf = _attn(xf, q, kT, v, ga, mb, bias, w_branch_att[l].astype(BF16),
                   w_out[l].astype(BF16), row(norm_xattn_g[l]), w_xq[l].astype(BF16),
                   kmT, vm, w_xo[l].astype(BF16), s)
        assert depth == 1
        xf = _ffn(xf, row(norm_ffn_g[l]), w_ffn_in[l].astype(BF16), w_ffn_out[l].astype(BF16),
                  row(norm_final_g))
    return xf.reshape(b, s, d)
```

```python
import functools

import jax
import jax.numpy as jnp
import numpy as np
from jax import lax
from jax.experimental import pallas as pl
from jax.experimental.pallas import tpu as pltpu

CHUNK = 64
N_PREV_CHUNKS = 8
ATT_HEADS = 8
ATT_HEAD_DIM = 64
ATT_WIDTH = ATT_HEADS * ATT_HEAD_DIM
REL_CLIP = 128
SG_BLOCK = 128
SG_GROUPS = 8
SG_GROUP_DIM = 64
SG_WIDTH = SG_GROUPS * SG_GROUP_DIM
XATT_HEADS = 4
EPS = 1e-6
NEG_INF = -1e30
LOG2E = 1.4426950408889634

LANES = 128
MXU_WIDTH = 256
VMEM_LIMIT_BYTES = 62 * 1024 * 1024

TOKEN_TILE = 1024
TAIL_ROWS = 512
INPROJ_TILE = 1024
FFN_TILE = 1024
FFN_SUB = 512
Q_SUB = 2 * CHUNK
PREV_KEYS = N_PREV_CHUNKS * CHUNK
WINDOW = PREV_KEYS + Q_SUB
FF_CHUNK = 256
SCORE_LOOKAHEAD = 2

F32 = jnp.float32
BF16 = jnp.bfloat16
_NT = (((1,), (1,)), ((), ()))


def _dot(a, b):
    return jnp.dot(a, b, preferred_element_type=F32)


def _dot_nt(a, b):
    return lax.dot_general(a, b, _NT, preferred_element_type=F32)


def _rmsnorm(x, g):
    return x * lax.rsqrt(jnp.mean(x * x, axis=-1, keepdims=True) + EPS) * g


def _const_spec(shape):
    zeros = (0,) * len(shape)
    return pl.BlockSpec(shape, lambda *_: zeros, pipeline_mode=pl.Buffered(1))


def _params():
    return pltpu.CompilerParams(dimension_semantics=("arbitrary",),
                                vmem_limit_bytes=VMEM_LIMIT_BYTES)


def _bias_kernel(rb_ref, o_ref):
    near_cols = Q_SUB + REL_CLIP
    far_cols = WINDOW - near_cols
    width = near_cols + Q_SUB
    c = lax.broadcasted_iota(jnp.int32, (1, width), 1)
    idx = jnp.clip(near_cols - c, -REL_CLIP, REL_CLIP) + REL_CLIP
    diag = jnp.zeros((ATT_HEADS, width), F32)
    for j in range(2 * REL_CLIP + 1):
        diag = jnp.where(idx == j, rb_ref[:, j:j + 1], diag)
    diag = diag * LOG2E

    row = lax.broadcasted_iota(jnp.int32, (Q_SUB, width), 0)
    qc = lax.broadcasted_iota(jnp.int32, (Q_SUB, WINDOW), 0) // CHUNK
    kc = lax.broadcasted_iota(jnp.int32, (Q_SUB, WINDOW), 1) // CHUNK
    valid = (kc >= qc) & (kc <= qc + N_PREV_CHUNKS)
    for h in range(ATT_HEADS):
        t = jnp.broadcast_to(diag[h:h + 1, :], (Q_SUB, width))
        for bit in range(Q_SUB.bit_length() - 1):
            t = jnp.where(((row >> bit) & 1) == 1, pltpu.roll(t, 1 << bit, axis=1), t)
        far = jnp.broadcast_to(diag[h:h + 1, 0:1], (Q_SUB, far_cols))
        o_ref[h] = jnp.where(valid, jnp.concatenate([far, t[:, Q_SUB:]], axis=1), NEG_INF)
    o_ref[ATT_HEADS] = jnp.full((Q_SUB, WINDOW), NEG_INF, F32)


def _bias_table(rel_bias):
    return pl.pallas_call(
        _bias_kernel,
        out_shape=jax.ShapeDtypeStruct((ATT_HEADS + 1, Q_SUB, WINDOW), F32),
        in_specs=[pl.BlockSpec(memory_space=pltpu.VMEM)],
        out_specs=pl.BlockSpec(memory_space=pltpu.VMEM),
        name="bias_table",
    )(rel_bias)


def _memkv_kernel(mem_ref, g_ref, wkT_ref, wv_ref, kT_ref, v_ref):
    mn = _rmsnorm(mem_ref[0], g_ref[...]).astype(BF16)
    kT_ref[0] = _dot_nt(wkT_ref[...], mn).astype(BF16)
    v_ref[0] = _dot(mn, wv_ref[...]).astype(BF16)


def _mem_kv(mem, g, wkT, wv):
    b, m, d = mem.shape
    return pl.pallas_call(
        _memkv_kernel,
        grid=(b,),
        out_shape=(jax.ShapeDtypeStruct((b, d, m), BF16), jax.ShapeDtypeStruct((b, m, d), BF16)),
        in_specs=[pl.BlockSpec((1, m, d), lambda i: (i, 0, 0)),
                  _const_spec((1, d)), _const_spec((d, d)), _const_spec((d, d))],
        out_specs=(pl.BlockSpec((1, d, m), lambda i: (i, 0, 0)),
                   pl.BlockSpec((1, m, d), lambda i: (i, 0, 0))),
        compiler_params=_params(),
        name="mem_kv",
    )(mem, g, wkT, wv)


def _inproj_kernel(x_ref, g_ref, wq_ref, wk_ref, wv_ref, wu_ref, wvs_ref, wga_ref, wgb_ref,
                   avg_ref, lng_ref, lnb_ref, sgw_ref, sgb_ref, wbsg_ref,
                   q_ref, kT_ref, v_ref, ga_ref, mb_ref, wm_scr, ysg_scr):
    pair_rows = 2 * SG_BLOCK

    @pl.when(pl.program_id(0) == 0)
    def _():
        t = lax.broadcasted_iota(jnp.int32, (pair_rows, SG_BLOCK), 0) % SG_BLOCK
        s = lax.broadcasted_iota(jnp.int32, (pair_rows, SG_BLOCK), 1)
        mask = (s // CHUNK) <= (t // CHUNK)
        for j in range(SG_GROUPS // 2):
            wm_scr[j] = jnp.where(mask, sgw_ref[j], 0.0).astype(BF16)

    hb = _rmsnorm(x_ref[...], g_ref[...]).astype(BF16)
    vs_raw = _dot(hb, wvs_ref[...])
    u_raw = _dot(hb, wu_ref[...])
    q_ref[...] = (_dot(hb, wq_ref[...]) * (ATT_HEAD_DIM ** -0.5 * LOG2E)).astype(BF16)
    vs = jax.nn.gelu(vs_raw)

    def group_mean(t):
        tb = t.astype(BF16)
        w = avg_ref.shape[0]
        return jnp.concatenate([_dot(tb[:, c:c + w], avg_ref[...])
                                for c in range(0, SG_WIDTH, w)], axis=1)

    mean = group_mean(vs)
    kT_ref[...] = lax.dot_general(wk_ref[...], hb, (((0,), (1,)), ((), ())),
                                  preferred_element_type=F32).astype(BF16)
    dev = vs - mean
    var = group_mean(dev * dev)
    v_ref[...] = _dot(hb, wv_ref[...]).astype(BF16)
    ga_raw = _dot(hb, wga_ref[...])
    vln = (dev * lax.rsqrt(var + EPS) * lng_ref[...] + lnb_ref[...]).astype(BF16)
    u = jax.nn.gelu(u_raw)

    lane = lax.broadcasted_iota(jnp.int32, (SG_BLOCK, LANES), 1)
    for n in range(x_ref.shape[0] // SG_BLOCK):
        rows = slice(n * SG_BLOCK, (n + 1) * SG_BLOCK)
        for j in range(SG_GROUPS // 2):
            cols = slice(j * LANES, (j + 1) * LANES)
            r = _dot(wm_scr[j], vln[rows, cols])
            sv = jnp.where(lane < SG_GROUP_DIM, r[:SG_BLOCK], r[SG_BLOCK:]) + sgb_ref[:, cols]
            ysg_scr[rows, cols] = (u[rows, cols] * sv).astype(BF16)

    gate_b = jax.nn.sigmoid(_dot(hb, wgb_ref[...]))
    mb_ref[...] = (gate_b * _dot(ysg_scr[...], wbsg_ref[...])).astype(BF16)
    ga_ref[...] = jax.nn.sigmoid(ga_raw).astype(BF16)


def _inproj(x, g, wq, wk, wv, wu, wvs, wga, wgb, avg, lng, lnb, sgw, sgb, wbsg):
    n, d = x.shape
    tm = INPROJ_TILE
    assert tm % SG_BLOCK == 0
    tile = lambda w: pl.BlockSpec((tm, w), lambda i: (i, 0))
    return pl.pallas_call(
        _inproj_kernel,
        grid=(n // tm,),
        out_shape=(jax.ShapeDtypeStruct((n, ATT_WIDTH), BF16),
                   jax.ShapeDtypeStruct((ATT_WIDTH, n), BF16),
                   jax.ShapeDtypeStruct((n, ATT_WIDTH), BF16),
                   jax.ShapeDtypeStruct((n, d), BF16),
                   jax.ShapeDtypeStruct((n, d), BF16)),
        in_specs=[tile(d), _const_spec((1, d)),
                  _const_spec(wq.shape), _const_spec(wk.shape), _const_spec(wv.shape),
                  _const_spec(wu.shape), _const_spec(wvs.shape), _const_spec(wga.shape),
                  _const_spec(wgb.shape), _const_spec(avg.shape), _const_spec(lng.shape),
                  _const_spec(lnb.shape), _const_spec(sgw.shape), _const_spec(sgb.shape),
                  _const_spec(wbsg.shape)],
        out_specs=(tile(ATT_WIDTH), pl.BlockSpec((ATT_WIDTH, tm), lambda i: (0, i)),
                   tile(ATT_WIDTH), tile(d), tile(d)),
        scratch_shapes=[pltpu.VMEM((SG_GROUPS // 2, 2 * SG_BLOCK, SG_BLOCK), BF16),
                        pltpu.VMEM((tm, SG_WIDTH), BF16)],
        compiler_params=_params(),
        name="inproj_sgu",
    )(x, g, wq, wk, wv, wu, wvs, wga, wgb, avg, lng, lnb, sgw, sgb, wbsg)


def _attn_kernel(tiles_per_seq, x_ref, q_ref, kTp_ref, kTc_ref, vp_ref, vc_ref, ga_ref, mb_ref,
                 bias_ref, wbatt_ref, wout_ref, gx_ref, wxq_ref, kmT_ref, vm_ref, wxo_ref,
                 o_ref, kT_scr, v_scr, yatt_scr, xo_scr):
    tm = x_ref.shape[0]
    d = x_ref.shape[1]
    first_tile = (pl.program_id(0) % tiles_per_seq) == 0

    kcat = jnp.concatenate([kTp_ref[...], kTc_ref[...]], axis=1)
    low_rows = (lax.broadcasted_iota(jnp.int32, kcat.shape, 0) % LANES) < ATT_HEAD_DIM
    kT_scr[0] = jnp.where(low_rows, kcat, jnp.zeros_like(kcat))
    kT_scr[1] = jnp.where(low_rows, jnp.zeros_like(kcat), kcat)
    vcat = jnp.concatenate([vp_ref[...], vc_ref[...]], axis=0)
    low_half = (lax.broadcasted_iota(jnp.int32, vcat.shape, 1) % LANES) < ATT_HEAD_DIM
    v_scr[0] = jnp.where(low_half, vcat, jnp.ones_like(vcat))
    v_scr[1] = jnp.where(low_half, jnp.ones_like(vcat), vcat)

    lane = lax.broadcasted_iota(jnp.int32, (Q_SUB, LANES), 1)
    n_blocks = WINDOW // LANES

    def window(j, h):
        q0 = j * Q_SUB
        cols = slice((h // 2) * LANES, (h // 2 + 1) * LANES)
        return q0, q0, cols

    def scores(j, p):
        q0, w0, cols = window(j, 2 * p)
        qp = q_ref[q0:q0 + Q_SUB, cols]
        kw = jnp.concatenate([kT_scr[0, cols, w0:w0 + WINDOW], kT_scr[1, cols, w0:w0 + WINDOW]],
                             axis=1)
        s = _dot(qp, kw)
        heads = []
        for hh in range(2):
            h = 2 * p + hh
            blocks = []
            for b in range(n_blocks):
                before_start = (b + 1) * LANES <= PREV_KEYS - q0
                entry = jnp.where(first_tile, ATT_HEADS, h) if before_start else h
                blk = slice(b * LANES, (b + 1) * LANES)
                sblk = slice(hh * WINDOW + b * LANES, hh * WINDOW + (b + 1) * LANES)
                blocks.append(s[:, sblk] + bias_ref[entry, :, blk])
            heads.append(jnp.concatenate(blocks, axis=1))
        return heads

    def attend(j, h, s):
        _, w0, cols = window(j, h)
        p = jnp.exp2(s - jnp.max(s, axis=-1, keepdims=True)).astype(BF16)
        o = _dot(p, v_scr[h % 2, w0:w0 + WINDOW, cols])
        return o / pltpu.roll(o, ATT_HEAD_DIM, axis=1)

    steps = [(j, p) for j in range(tm // Q_SUB) for p in range(ATT_HEADS // 2)]
    pending = [scores(*st) for st in steps[:SCORE_LOOKAHEAD]]
    for k, (j, p) in enumerate(steps):
        if k + SCORE_LOOKAHEAD < len(steps):
            pending.append(scores(*steps[k + SCORE_LOOKAHEAD]))
        s_even, s_odd = pending.pop(0)
        o_even = attend(j, 2 * p, s_even)
        o_odd = attend(j, 2 * p + 1, s_odd)
        q0, _, cols = window(j, 2 * p)
        y = jnp.where(lane < ATT_HEAD_DIM, o_even, o_odd)
        yatt_scr[q0:q0 + Q_SUB, cols] = y.astype(BF16)

    groups = [slice(r, r + TAIL_ROWS) for r in range(0, tm, TAIL_ROWS)]
    xd = d // XATT_HEADS
    heads = [slice(h * xd, (h + 1) * xd) for h in range(XATT_HEADS)]
    att = [_dot(yatt_scr[g, :], wbatt_ref[...]) for g in groups]
    x1 = []
    for g, a in zip(groups, att):
        merged = ga_ref[g, :].astype(F32) * a + mb_ref[g, :].astype(F32)
        x1.append(x_ref[g, :] + _dot(merged.astype(BF16), wout_ref[...]))
    qx = [(_dot(_rmsnorm(v, gx_ref[...]).astype(BF16), wxq_ref[...])
           * (xd ** -0.5 * LOG2E)).astype(BF16) for v in x1]
    xs = [[_dot(q[:, cols], kmT_ref[0, cols, :]) for q in qx] for cols in heads]
    for gi, (g, v) in enumerate(zip(groups, x1)):
        for cols, per_group in zip(heads, xs):
            s = per_group[gi]
            e = jnp.exp2(s - jnp.max(s, axis=-1, keepdims=True))
            den = jnp.sum(e, axis=-1, keepdims=True)
            xo_scr[g, cols] = (_dot(e.astype(BF16), vm_ref[0, :, cols]) / den).astype(BF16)
        o_ref[g, :] = v + _dot(xo_scr[g, :], wxo_ref[...])


def _attn(x, q, kT, v, ga, mb, bias, wbatt, wout, gx, wxq, kmT, vm, wxo, seq):
    n, d = x.shape
    tm = TOKEN_TILE
    tps = seq // tm
    mem_len = vm.shape[1]
    tile = lambda w: pl.BlockSpec((tm, w), lambda i: (i, 0))
    prev = lambda i: jnp.maximum(i * (tm // PREV_KEYS) - 1, 0)
    return pl.pallas_call(
        functools.partial(_attn_kernel, tps),
        grid=(n // tm,),
        out_shape=jax.ShapeDtypeStruct((n, d), F32),
        in_specs=[tile(d), tile(ATT_WIDTH),
                  pl.BlockSpec((ATT_WIDTH, PREV_KEYS), lambda i: (0, prev(i))),
                  pl.BlockSpec((ATT_WIDTH, tm), lambda i: (0, i)),
                  pl.BlockSpec((PREV_KEYS, ATT_WIDTH), lambda i: (prev(i), 0)),
                  tile(ATT_WIDTH), tile(d), tile(d),
                  _const_spec(bias.shape), _const_spec(wbatt.shape), _const_spec(wout.shape),
                  _const_spec((1, d)), _const_spec(wxq.shape),
                  pl.BlockSpec((1, d, mem_len), lambda i: (i // tps, 0, 0)),
                  pl.BlockSpec((1, mem_len, d), lambda i: (i // tps, 0, 0)),
                  _const_spec(wxo.shape)],
        out_specs=tile(d),
        scratch_shapes=[pltpu.VMEM((2, ATT_WIDTH, PREV_KEYS + tm), BF16),
                        pltpu.VMEM((2, PREV_KEYS + tm, ATT_WIDTH), BF16),
                        pltpu.VMEM((tm, ATT_WIDTH), BF16),
                        pltpu.VMEM((tm, d), BF16)],
        compiler_params=_params(),
        name="attn_merge_xattn",
    )(x, q, kT, kT, v, v, ga, mb, bias, wbatt, wout, gx, wxq, kmT, vm, wxo)


def _ffn_kernel(x_ref, g_ref, wfi_ref, wfo_ref, gfin_ref, o_ref, a_scr):
    d_ff = wfo_ref.shape[0]
    subs = [slice(r, r + FFN_SUB) for r in range(0, x_ref.shape[0], FFN_SUB)]
    def finish(rows):
        x3 = x_ref[rows, :] + _dot(a_scr[rows, :], wfo_ref[...])
        o_ref[rows, :] = _rmsnorm(x3, gfin_ref[...])

    hb_next = _rmsnorm(x_ref[subs[0], :], g_ref[...]).astype(BF16)
    for r, rows in enumerate(subs):
        hb = hb_next
        for c in range(d_ff // FF_CHUNK):
            gate = _dot(hb, wfi_ref[:, c * FF_CHUNK:(c + 1) * FF_CHUNK])
            up = _dot(hb, wfi_ref[:, d_ff + c * FF_CHUNK:d_ff + (c + 1) * FF_CHUNK])
            a_scr[rows, c * FF_CHUNK:(c + 1) * FF_CHUNK] = (jax.nn.silu(gate) * up).astype(BF16)
            if c == 0 and r + 1 < len(subs):
                hb_next = _rmsnorm(x_ref[subs[r + 1], :], g_ref[...]).astype(BF16)
            if c == 0 and r > 0:
                finish(subs[r - 1])
    finish(subs[-1])


def _ffn(x, g, wfi, wfo, gfin):
    n, d = x.shape
    tm = FFN_TILE
    d_ff = wfo.shape[0]
    assert d_ff % FF_CHUNK == 0
    tile = pl.BlockSpec((tm, d), lambda i: (i, 0))
    return pl.pallas_call(
        _ffn_kernel,
        grid=(n // tm,),
        out_shape=jax.ShapeDtypeStruct((n, d), F32),
        in_specs=[tile, _const_spec((1, d)), _const_spec(wfi.shape), _const_spec(wfo.shape),
                  _const_spec((1, d))],
        out_specs=tile,
        scratch_shapes=[pltpu.VMEM((tm, d_ff), BF16)],
        compiler_params=_params(),
        name="ffn_final_norm",
    )(x, g, wfi, wfo, gfin)


def kernel(x, mem, norm_mix_g, w_in, rel_bias, sg_ln_g, sg_ln_b, sg_w, sg_b, w_branch_att,
           w_branch_sg, w_out, norm_xattn_g, norm_mem_g, w_xq, w_xkv, w_xo, norm_ffn_g,
           w_ffn_in, w_ffn_out, norm_final_g):
    b, s, d = x.shape
    depth = w_in.shape[0]
    assert s % TOKEN_TILE == 0 and TOKEN_TILE % PREV_KEYS == 0 and TOKEN_TILE % Q_SUB == 0
    assert (b * s) % FFN_TILE == 0 and (b * s) % INPROJ_TILE == 0
    row = lambda a: a.reshape(1, -1).astype(F32)
    grp = np.arange(MXU_WIDTH) // SG_GROUP_DIM
    avg = jnp.asarray((grp[:, None] == grp[None, :]) / SG_GROUP_DIM, BF16)

    xf = x.reshape(b * s, d)
    for l in range(depth):
        bounds = np.cumsum([0, ATT_WIDTH, ATT_WIDTH, ATT_WIDTH, SG_WIDTH, SG_WIDTH, d, d])
        wq, wk, wv, wu, wvs, wga, wgb = (
            w_in[l, :, int(lo):int(hi)].astype(BF16) for lo, hi in zip(bounds[:-1], bounds[1:]))
        sgw = sg_w[l].reshape(SG_GROUPS // 2, 2 * SG_BLOCK, SG_BLOCK)
        sgb = jnp.repeat(sg_b[l].T, SG_GROUP_DIM, axis=1)

        bias = _bias_table(rel_bias[l])
        kmT, vm = _mem_kv(mem, row(norm_mem_g[l]), w_xkv[l, :, :d].T.astype(BF16),
                          w_xkv[l, :, d:].astype(BF16))
        q, kT, v, ga, mb = _inproj(
            xf, row(norm_mix_g[l]), wq, wk, wv, wu, wvs, wga, wgb, avg,
            row(sg_ln_g[l]), row(sg_ln_b[l]), sgw, sgb, w_branch_sg[l].astype(BF16))
        xf = _attn(xf, q, kT, v, ga, mb, bias, w_branch_att[l].astype(BF16),
                   w_out[l].astype(BF16), row(norm_xattn_g[l]), w_xq[l].astype(BF16),
                   kmT, vm, w_xo[l].astype(BF16), s)
        assert depth == 1
        xf = _ffn(xf, row(norm_ffn_g[l]), w_ffn_in[l].astype(BF16), w_ffn_out[l].astype(BF16),
                  row(norm_final_g))
    return xf.reshape(b, s, d)
```

```python
import functools

import jax
import jax.numpy as jnp
import numpy as np
from jax import lax
from jax.experimental import pallas as pl
from jax.experimental.pallas import tpu as pltpu

CHUNK = 64
N_PREV_CHUNKS = 8
ATT_HEADS = 8
ATT_HEAD_DIM = 64
ATT_WIDTH = ATT_HEADS * ATT_HEAD_DIM
REL_CLIP = 128
SG_BLOCK = 128
SG_GROUPS = 8
SG_GROUP_DIM = 64
SG_WIDTH = SG_GROUPS * SG_GROUP_DIM
XATT_HEADS = 4
EPS = 1e-6
NEG_INF = -1e30
LOG2E = 1.4426950408889634

LANES = 128
MXU_WIDTH = 256
VMEM_LIMIT_BYTES = 62 * 1024 * 1024

TOKEN_TILE = 1024
TAIL_ROWS = 512
INPROJ_TILE = 1024
FFN_TILE = 1024
FFN_SUB = 256
Q_SUB = 2 * CHUNK
PREV_KEYS = N_PREV_CHUNKS * CHUNK
WINDOW = PREV_KEYS + Q_SUB
FF_CHUNK = 256
SCORE_LOOKAHEAD = 2

F32 = jnp.float32
BF16 = jnp.bfloat16
_NT = (((1,), (1,)), ((), ()))


def _dot(a, b):
    return jnp.dot(a, b, preferred_element_type=F32)


def _dot_nt(a, b):
    return lax.dot_general(a, b, _NT, preferred_element_type=F32)


def _rmsnorm(x, g):
    return x * lax.rsqrt(jnp.mean(x * x, axis=-1, keepdims=True) + EPS) * g


def _const_spec(shape):
    zeros = (0,) * len(shape)
    return pl.BlockSpec(shape, lambda *_: zeros, pipeline_mode=pl.Buffered(1))


def _params():
    return pltpu.CompilerParams(dimension_semantics=("arbitrary",),
                                vmem_limit_bytes=VMEM_LIMIT_BYTES)


def _bias_kernel(rb_ref, o_ref):
    near_cols = Q_SUB + REL_CLIP
    far_cols = WINDOW - near_cols
    width = near_cols + Q_SUB
    c = lax.broadcasted_iota(jnp.int32, (1, width), 1)
    idx = jnp.clip(near_cols - c, -REL_CLIP, REL_CLIP) + REL_CLIP
    diag = jnp.zeros((ATT_HEADS, width), F32)
    for j in range(2 * REL_CLIP + 1):
        diag = jnp.where(idx == j, rb_ref[:, j:j + 1], diag)
    diag = diag * LOG2E

    row = lax.broadcasted_iota(jnp.int32, (Q_SUB, width), 0)
    qc = lax.broadcasted_iota(jnp.int32, (Q_SUB, WINDOW), 0) // CHUNK
    kc = lax.broadcasted_iota(jnp.int32, (Q_SUB, WINDOW), 1) // CHUNK
    valid = (kc >= qc) & (kc <= qc + N_PREV_CHUNKS)
    for h in range(ATT_HEADS):
        t = jnp.broadcast_to(diag[h:h + 1, :], (Q_SUB, width))
        for bit in range(Q_SUB.bit_length() - 1):
            t = jnp.where(((row >> bit) & 1) == 1, pltpu.roll(t, 1 << bit, axis=1), t)
        far = jnp.broadcast_to(diag[h:h + 1, 0:1], (Q_SUB, far_cols))
        o_ref[h] = jnp.where(valid, jnp.concatenate([far, t[:, Q_SUB:]], axis=1), NEG_INF)
    o_ref[ATT_HEADS] = jnp.full((Q_SUB, WINDOW), NEG_INF, F32)


def _bias_table(rel_bias):
    return pl.pallas_call(
        _bias_kernel,
        out_shape=jax.ShapeDtypeStruct((ATT_HEADS + 1, Q_SUB, WINDOW), F32),
        in_specs=[pl.BlockSpec(memory_space=pltpu.VMEM)],
        out_specs=pl.BlockSpec(memory_space=pltpu.VMEM),
        name="bias_table",
    )(rel_bias)


def _memkv_kernel(mem_ref, g_ref, wkT_ref, wv_ref, kT_ref, v_ref):
    mn = _rmsnorm(mem_ref[0], g_ref[...]).astype(BF16)
    kT_ref[0] = _dot_nt(wkT_ref[...], mn).astype(BF16)
    v_ref[0] = _dot(mn, wv_ref[...]).astype(BF16)


def _mem_kv(mem, g, wkT, wv):
    b, m, d = mem.shape
    return pl.pallas_call(
        _memkv_kernel,
        grid=(b,),
        out_shape=(jax.ShapeDtypeStruct((b, d, m), BF16), jax.ShapeDtypeStruct((b, m, d), BF16)),
        in_specs=[pl.BlockSpec((1, m, d), lambda i: (i, 0, 0)),
                  _const_spec((1, d)), _const_spec((d, d)), _const_spec((d, d))],
        out_specs=(pl.BlockSpec((1, d, m), lambda i: (i, 0, 0)),
                   pl.BlockSpec((1, m, d), lambda i: (i, 0, 0))),
        compiler_params=_params(),
        name="mem_kv",
    )(mem, g, wkT, wv)


def _inproj_kernel(x_ref, g_ref, wq_ref, wk_ref, wv_ref, wu_ref, wvs_ref, wga_ref, wgb_ref,
                   avg_ref, lng_ref, lnb_ref, sgw_ref, sgb_ref, wbsg_ref,
                   q_ref, kT_ref, v_ref, ga_ref, mb_ref, wm_scr, ysg_scr):
    pair_rows = 2 * SG_BLOCK

    @pl.when(pl.program_id(0) == 0)
    def _():
        t = lax.broadcasted_iota(jnp.int32, (pair_rows, SG_BLOCK), 0) % SG_BLOCK
        s = lax.broadcasted_iota(jnp.int32, (pair_rows, SG_BLOCK), 1)
        mask = (s // CHUNK) <= (t // CHUNK)
        for j in range(SG_GROUPS // 2):
            wm_scr[j] = jnp.where(mask, sgw_ref[j], 0.0).astype(BF16)

    hb = _rmsnorm(x_ref[...], g_ref[...]).astype(BF16)
    vs_raw = _dot(hb, wvs_ref[...])
    u_raw = _dot(hb, wu_ref[...])
    q_ref[...] = (_dot(hb, wq_ref[...]) * (ATT_HEAD_DIM ** -0.5 * LOG2E)).astype(BF16)
    vs = jax.nn.gelu(vs_raw)

    def group_mean(t):
        tb = t.astype(BF16)
        w = avg_ref.shape[0]
        return jnp.concatenate([_dot(tb[:, c:c + w], avg_ref[...])
                                for c in range(0, SG_WIDTH, w)], axis=1)

    mean = group_mean(vs)
    kT_ref[...] = _dot(hb, wk_ref[...]).T.astype(BF16)
    dev = vs - mean
    var = group_mean(dev * dev)
    v_ref[...] = _dot(hb, wv_ref[...]).astype(BF16)
    ga_raw = _dot(hb, wga_ref[...])
    vln = (dev * lax.rsqrt(var + EPS) * lng_ref[...] + lnb_ref[...]).astype(BF16)
    u = jax.nn.gelu(u_raw)

    lane = lax.broadcasted_iota(jnp.int32, (SG_BLOCK, LANES), 1)
    for n in range(x_ref.shape[0] // SG_BLOCK):
        rows = slice(n * SG_BLOCK, (n + 1) * SG_BLOCK)
        for j in range(SG_GROUPS // 2):
            cols = slice(j * LANES, (j + 1) * LANES)
            r = _dot(wm_scr[j], vln[rows, cols])
            sv = jnp.where(lane < SG_GROUP_DIM, r[:SG_BLOCK], r[SG_BLOCK:]) + sgb_ref[:, cols]
            ysg_scr[rows, cols] = (u[rows, cols] * sv).astype(BF16)

    gate_b = jax.nn.sigmoid(_dot(hb, wgb_ref[...]))
    mb_ref[...] = (gate_b * _dot(ysg_scr[...], wbsg_ref[...])).astype(BF16)
    ga_ref[...] = jax.nn.sigmoid(ga_raw).astype(BF16)


def _inproj(x, g, wq, wk, wv, wu, wvs, wga, wgb, avg, lng, lnb, sgw, sgb, wbsg):
    n, d = x.shape
    tm = INPROJ_TILE
    assert tm % SG_BLOCK == 0
    tile = lambda w: pl.BlockSpec((tm, w), lambda i: (i, 0))
    return pl.pallas_call(
        _inproj_kernel,
        grid=(n // tm,),
        out_shape=(jax.ShapeDtypeStruct((n, ATT_WIDTH), BF16),
                   jax.ShapeDtypeStruct((ATT_WIDTH, n), BF16),
                   jax.ShapeDtypeStruct((n, ATT_WIDTH), BF16),
                   jax.ShapeDtypeStruct((n, d), BF16),
                   jax.ShapeDtypeStruct((n, d), BF16)),
        in_specs=[tile(d), _const_spec((1, d)),
                  _const_spec(wq.shape), _const_spec(wk.shape), _const_spec(wv.shape),
                  _const_spec(wu.shape), _const_spec(wvs.shape), _const_spec(wga.shape),
                  _const_spec(wgb.shape), _const_spec(avg.shape), _const_spec(lng.shape),
                  _const_spec(lnb.shape), _const_spec(sgw.shape), _const_spec(sgb.shape),
                  _const_spec(wbsg.shape)],
        out_specs=(tile(ATT_WIDTH), pl.BlockSpec((ATT_WIDTH, tm), lambda i: (0, i)),
                   tile(ATT_WIDTH), tile(d), tile(d)),
        scratch_shapes=[pltpu.VMEM((SG_GROUPS // 2, 2 * SG_BLOCK, SG_BLOCK), BF16),
                        pltpu.VMEM((tm, SG_WIDTH), BF16)],
        compiler_params=_params(),
        name="inproj_sgu",
    )(x, g, wq, wk, wv, wu, wvs, wga, wgb, avg, lng, lnb, sgw, sgb, wbsg)


def _attn_kernel(tiles_per_seq, x_ref, q_ref, kTp_ref, kTc_ref, vp_ref, vc_ref, ga_ref, mb_ref,
                 bias_ref, wbatt_ref, wout_ref, gx_ref, wxq_ref, kmT_ref, vm_ref, wxo_ref,
                 o_ref, kT_scr, v_scr, yatt_scr, xo_scr):
    tm = x_ref.shape[0]
    d = x_ref.shape[1]
    first_tile = (pl.program_id(0) % tiles_per_seq) == 0

    kcat = jnp.concatenate([kTp_ref[...], kTc_ref[...]], axis=1)
    low_rows = (lax.broadcasted_iota(jnp.int32, kcat.shape, 0) % LANES) < ATT_HEAD_DIM
    kT_scr[0] = jnp.where(low_rows, kcat, jnp.zeros_like(kcat))
    kT_scr[1] = jnp.where(low_rows, jnp.zeros_like(kcat), kcat)
    vcat = jnp.concatenate([vp_ref[...], vc_ref[...]], axis=0)
    low_half = (lax.broadcasted_iota(jnp.int32, vcat.shape, 1) % LANES) < ATT_HEAD_DIM
    v_scr[0] = jnp.where(low_half, vcat, jnp.ones_like(vcat))
    v_scr[1] = jnp.where(low_half, jnp.ones_like(vcat), vcat)

    lane = lax.broadcasted_iota(jnp.int32, (Q_SUB, LANES), 1)
    n_blocks = WINDOW // LANES

    def window(j, h):
        q0 = j * Q_SUB
        cols = slice((h // 2) * LANES, (h // 2 + 1) * LANES)
        return q0, q0, cols

    def scores(j, p):
        q0, w0, cols = window(j, 2 * p)
        qp = q_ref[q0:q0 + Q_SUB, cols]
        kw = jnp.concatenate([kT_scr[0, cols, w0:w0 + WINDOW], kT_scr[1, cols, w0:w0 + WINDOW]],
                             axis=1)
        s = _dot(qp, kw)
        heads = []
        for hh in range(2):
            h = 2 * p + hh
            blocks = []
            for b in range(n_blocks):
                before_start = (b + 1) * LANES <= PREV_KEYS - q0
                entry = jnp.where(first_tile, ATT_HEADS, h) if before_start else h
                blk = slice(b * LANES, (b + 1) * LANES)
                sblk = slice(hh * WINDOW + b * LANES, hh * WINDOW + (b + 1) * LANES)
                blocks.append(s[:, sblk] + bias_ref[entry, :, blk])
            heads.append(jnp.concatenate(blocks, axis=1))
        return heads

    def attend(j, h, s):
        _, w0, cols = window(j, h)
        p = jnp.exp2(s - jnp.max(s, axis=-1, keepdims=True)).astype(BF16)
        o = _dot(p, v_scr[h % 2, w0:w0 + WINDOW, cols])
        return o / pltpu.roll(o, ATT_HEAD_DIM, axis=1)

    steps = [(j, p) for j in range(tm // Q_SUB) for p in range(ATT_HEADS // 2)]
    pending = [scores(*st) for st in steps[:SCORE_LOOKAHEAD]]
    for k, (j, p) in enumerate(steps):
        if k + SCORE_LOOKAHEAD < len(steps):
            pending.append(scores(*steps[k + SCORE_LOOKAHEAD]))
        s_even, s_odd = pending.pop(0)
        o_even = attend(j, 2 * p, s_even)
        o_odd = attend(j, 2 * p + 1, s_odd)
        q0, _, cols = window(j, 2 * p)
        y = jnp.where(lane < ATT_HEAD_DIM, o_even, o_odd)
        yatt_scr[q0:q0 + Q_SUB, cols] = y.astype(BF16)

    groups = [slice(r, r + TAIL_ROWS) for r in range(0, tm, TAIL_ROWS)]
    xd = d // XATT_HEADS
    heads = [slice(h * xd, (h + 1) * xd) for h in range(XATT_HEADS)]
    att = [_dot(yatt_scr[g, :], wbatt_ref[...]) for g in groups]
    x1 = []
    for g, a in zip(groups, att):
        merged = ga_ref[g, :].astype(F32) * a + mb_ref[g, :].astype(F32)
        x1.append(x_ref[g, :] + _dot(merged.astype(BF16), wout_ref[...]))
    qx = [(_dot(_rmsnorm(v, gx_ref[...]).astype(BF16), wxq_ref[...])
           * (xd ** -0.5 * LOG2E)).astype(BF16) for v in x1]
    xs = [[_dot(q[:, cols], kmT_ref[0, cols, :]) for q in qx] for cols in heads]
    for gi, (g, v) in enumerate(zip(groups, x1)):
        for cols, per_group in zip(heads, xs):
            s = per_group[gi]
            e = jnp.exp2(s - jnp.max(s, axis=-1, keepdims=True))
            den = jnp.sum(e, axis=-1, keepdims=True)
            xo_scr[g, cols] = (_dot(e.astype(BF16), vm_ref[0, :, cols]) / den).astype(BF16)
        o_ref[g, :] = v + _dot(xo_scr[g, :], wxo_ref[...])


def _attn(x, q, kT, v, ga, mb, bias, wbatt, wout, gx, wxq, kmT, vm, wxo, seq):
    n, d = x.shape
    tm = TOKEN_TILE
    tps = seq // tm
    mem_len = vm.shape[1]
    tile = lambda w: pl.BlockSpec((tm, w), lambda i: (i, 0))
    prev = lambda i: jnp.maximum(i * (tm // PREV_KEYS) - 1, 0)
    return pl.pallas_call(
        functools.partial(_attn_kernel, tps),
        grid=(n // tm,),
        out_shape=jax.ShapeDtypeStruct((n, d), F32),
        in_specs=[tile(d), tile(ATT_WIDTH),
                  pl.BlockSpec((ATT_WIDTH, PREV_KEYS), lambda i: (0, prev(i))),
                  pl.BlockSpec((ATT_WIDTH, tm), lambda i: (0, i)),
                  pl.BlockSpec((PREV_KEYS, ATT_WIDTH), lambda i: (prev(i), 0)),
                  tile(ATT_WIDTH), tile(d), tile(d),
                  _const_spec(bias.shape), _const_spec(wbatt.shape), _const_spec(wout.shape),
                  _const_spec((1, d)), _const_spec(wxq.shape),
                  pl.BlockSpec((1, d, mem_len), lambda i: (i // tps, 0, 0)),
                  pl.BlockSpec((1, mem_len, d), lambda i: (i // tps, 0, 0)),
                  _const_spec(wxo.shape)],
        out_specs=tile(d),
        scratch_shapes=[pltpu.VMEM((2, ATT_WIDTH, PREV_KEYS + tm), BF16),
                        pltpu.VMEM((2, PREV_KEYS + tm, ATT_WIDTH), BF16),
                        pltpu.VMEM((tm, ATT_WIDTH), BF16),
                        pltpu.VMEM((tm, d), BF16)],
        compiler_params=_params(),
        name="attn_merge_xattn",
    )(x, q, kT, kT, v, v, ga, mb, bias, wbatt, wout, gx, wxq, kmT, vm, wxo)


def _ffn_kernel(x_ref, g_ref, wfi_ref, wfo_ref, gfin_ref, o_ref, a_scr):
    d_ff = wfo_ref.shape[0]
    subs = [slice(r, r + FFN_SUB) for r in range(0, x_ref.shape[0], FFN_SUB)]
    def finish(rows):
        x3 = x_ref[rows, :] + _dot(a_scr[rows, :], wfo_ref[...])
        o_ref[rows, :] = _rmsnorm(x3, gfin_ref[...])

    hb_next = _rmsnorm(x_ref[subs[0], :], g_ref[...]).astype(BF16)
    for r, rows in enumerate(subs):
        hb = hb_next
        for c in range(d_ff // FF_CHUNK):
            gate = _dot(hb, wfi_ref[:, c * FF_CHUNK:(c + 1) * FF_CHUNK])
            up = _dot(hb, wfi_ref[:, d_ff + c * FF_CHUNK:d_ff + (c + 1) * FF_CHUNK])
            a_scr[rows, c * FF_CHUNK:(c + 1) * FF_CHUNK] = (jax.nn.silu(gate) * up).astype(BF16)
            if c == 0 and r + 1 < len(subs):
                hb_next = _rmsnorm(x_ref[subs[r + 1], :], g_ref[...]).astype(BF16)
            if c == 0 and r > 0:
                finish(subs[r - 1])
    finish(subs[-1])


def _ffn(x, g, wfi, wfo, gfin):
    n, d = x.shape
    tm = FFN_TILE
    d_ff = wfo.shape[0]
    assert d_ff % FF_CHUNK == 0
    tile = pl.BlockSpec((tm, d), lambda i: (i, 0))
    return pl.pallas_call(
        _ffn_kernel,
        grid=(n // tm,),
        out_shape=jax.ShapeDtypeStruct((n, d), F32),
        in_specs=[tile, _const_spec((1, d)), _const_spec(wfi.shape), _const_spec(wfo.shape),
                  _const_spec((1, d))],
        out_specs=tile,
        scratch_shapes=[pltpu.VMEM((tm, d_ff), BF16)],
        compiler_params=_params(),
        name="ffn_final_norm",
    )(x, g, wfi, wfo, gfin)


def kernel(x, mem, norm_mix_g, w_in, rel_bias, sg_ln_g, sg_ln_b, sg_w, sg_b, w_branch_att,
           w_branch_sg, w_out, norm_xattn_g, norm_mem_g, w_xq, w_xkv, w_xo, norm_ffn_g,
           w_ffn_in, w_ffn_out, norm_final_g):
    b, s, d = x.shape
    depth = w_in.shape[0]
    assert s % TOKEN_TILE == 0 and TOKEN_TILE % PREV_KEYS == 0 and TOKEN_TILE % Q_SUB == 0
    assert (b * s) % FFN_TILE == 0 and (b * s) % INPROJ_TILE == 0
    row = lambda a: a.reshape(1, -1).astype(F32)
    grp = np.arange(MXU_WIDTH) // SG_GROUP_DIM
    avg = jnp.asarray((grp[:, None] == grp[None, :]) / SG_GROUP_DIM, BF16)

    xf = x.reshape(b * s, d)
    for l in range(depth):
        bounds = np.cumsum([0, ATT_WIDTH, ATT_WIDTH, ATT_WIDTH, SG_WIDTH, SG_WIDTH, d, d])
        wq, wk, wv, wu, wvs, wga, wgb = (
            w_in[l, :, int(lo):int(hi)].astype(BF16) for lo, hi in zip(bounds[:-1], bounds[1:]))
        sgw = sg_w[l].reshape(SG_GROUPS // 2, 2 * SG_BLOCK, SG_BLOCK)
        sgb = jnp.repeat(sg_b[l].T, SG_GROUP_DIM, axis=1)

        bias = _bias_table(rel_bias[l])
        kmT, vm = _mem_kv(mem, row(norm_mem_g[l]), w_xkv[l, :, :d].T.astype(BF16),
                          w_xkv[l, :, d:].astype(BF16))
        q, kT, v, ga, mb = _inproj(
            xf, row(norm_mix_g[l]), wq, wk, wv, wu, wvs, wga, wgb, avg,
            row(sg_ln_g[l]), row(sg_ln_b[l]), sgw, sgb, w_branch_sg[l].astype(BF16))
        xf = _attn(xf, q, kT, v, ga, mb, bias, w_branch_att[l].astype(BF16),
                   w_out[l].astype(BF16), row(norm_xattn_g[l]), w_xq[l].astype(BF16),
                   kmT, vm, w_xo[l].astype(BF16), s)
        assert depth == 1
        xf = _ffn(xf, row(norm_ffn_g[l]), w_ffn_in[l].astype(BF16), w_ffn_out[l].astype(BF16),
                  row(norm_final_g))
    return xf.reshape(b, s, d)
```

```python
import functools

import jax
import jax.numpy as jnp
import numpy as np
from jax import lax
from jax.experimental import pallas as pl
from jax.experimental.pallas import tpu as pltpu

CHUNK = 64
N_PREV_CHUNKS = 8
ATT_HEADS = 8
ATT_HEAD_DIM = 64
ATT_WIDTH = ATT_HEADS * ATT_HEAD_DIM
REL_CLIP = 128
SG_BLOCK = 128
SG_GROUPS = 8
SG_GROUP_DIM = 64
SG_WIDTH = SG_GROUPS * SG_GROUP_DIM
XATT_HEADS = 4
EPS = 1e-6
NEG_INF = -1e30
LOG2E = 1.4426950408889634

LANES = 128
MXU_WIDTH = 256
VMEM_LIMIT_BYTES = 62 * 1024 * 1024

TOKEN_TILE = 1024
TAIL_ROWS = 512
INPROJ_TILE = 1024
FFN_TILE = 1024
FFN_SUB = 512
Q_SUB = 2 * CHUNK
PREV_KEYS = N_PREV_CHUNKS * CHUNK
WINDOW = PREV_KEYS + Q_SUB
FF_CHUNK = 256
SCORE_LOOKAHEAD = 2

F32 = jnp.float32
BF16 = jnp.bfloat16
_NT = (((1,), (1,)), ((), ()))


def _dot(a, b):
    return jnp.dot(a, b, preferred_element_type=F32)


def _dot_nt(a, b):
    return lax.dot_general(a, b, _NT, preferred_element_type=F32)


def _rmsnorm(x, g):
    return x * lax.rsqrt(jnp.mean(x * x, axis=-1, keepdims=True) + EPS) * g


def _const_spec(shape):
    zeros = (0,) * len(shape)
    return pl.BlockSpec(shape, lambda *_: zeros, pipeline_mode=pl.Buffered(1))


def _params():
    return pltpu.CompilerParams(dimension_semantics=("arbitrary",),
                                vmem_limit_bytes=VMEM_LIMIT_BYTES)


def _bias_kernel(rb_ref, o_ref):
    near_cols = Q_SUB + REL_CLIP
    far_cols = WINDOW - near_cols
    width = near_cols + Q_SUB
    c = lax.broadcasted_iota(jnp.int32, (1, width), 1)
    idx = jnp.clip(near_cols - c, -REL_CLIP, REL_CLIP) + REL_CLIP
    diag = jnp.zeros((ATT_HEADS, width), F32)
    for j in range(2 * REL_CLIP + 1):
        diag = jnp.where(idx == j, rb_ref[:, j:j + 1], diag)
    diag = diag * LOG2E

    row = lax.broadcasted_iota(jnp.int32, (Q_SUB, width), 0)
    qc = lax.broadcasted_iota(jnp.int32, (Q_SUB, WINDOW), 0) // CHUNK
    kc = lax.broadcasted_iota(jnp.int32, (Q_SUB, WINDOW), 1) // CHUNK
    valid = (kc >= qc) & (kc <= qc + N_PREV_CHUNKS)
    for h in range(ATT_HEADS):
        t = jnp.broadcast_to(diag[h:h + 1, :], (Q_SUB, width))
        for bit in range(Q_SUB.bit_length() - 1):
            t = jnp.where(((row >> bit) & 1) == 1, pltpu.roll(t, 1 << bit, axis=1), t)
        far = jnp.broadcast_to(diag[h:h + 1, 0:1], (Q_SUB, far_cols))
        o_ref[h] = jnp.where(valid, jnp.concatenate([far, t[:, Q_SUB:]], axis=1), NEG_INF)
    o_ref[ATT_HEADS] = jnp.full((Q_SUB, WINDOW), NEG_INF, F32)


def _bias_table(rel_bias):
    return pl.pallas_call(
        _bias_kernel,
        out_shape=jax.ShapeDtypeStruct((ATT_HEADS + 1, Q_SUB, WINDOW), F32),
        in_specs=[pl.BlockSpec(memory_space=pltpu.VMEM)],
        out_specs=pl.BlockSpec(memory_space=pltpu.VMEM),
        name="bias_table",
    )(rel_bias)


def _memkv_kernel(mem_ref, g_ref, wkT_ref, wv_ref, kT_ref, v_ref):
    mn = _rmsnorm(mem_ref[0], g_ref[...]).astype(BF16)
    kT_ref[0] = _dot_nt(wkT_ref[...], mn).astype(BF16)
    v_ref[0] = _dot(mn, wv_ref[...]).astype(BF16)


def _mem_kv(mem, g, wkT, wv):
    b, m, d = mem.shape
    return pl.pallas_call(
        _memkv_kernel,
        grid=(b,),
        out_shape=(jax.ShapeDtypeStruct((b, d, m), BF16), jax.ShapeDtypeStruct((b, m, d), BF16)),
        in_specs=[pl.BlockSpec((1, m, d), lambda i: (i, 0, 0)),
                  _const_spec((1, d)), _const_spec((d, d)), _const_spec((d, d))],
        out_specs=(pl.BlockSpec((1, d, m), lambda i: (i, 0, 0)),
                   pl.BlockSpec((1, m, d), lambda i: (i, 0, 0))),
        compiler_params=pltpu.CompilerParams(
            dimension_semantics=("arbitrary",), vmem_limit_bytes=VMEM_LIMIT_BYTES,
            allow_input_fusion=[False, False, True, True]),
        name="mem_kv",
    )(mem, g, wkT, wv)


def _inproj_kernel(x_ref, g_ref, wq_ref, wk_ref, wv_ref, wu_ref, wvs_ref, wga_ref, wgb_ref,
                   avg_ref, lng_ref, lnb_ref, sgw_ref, sgb_ref, wbsg_ref,
                   q_ref, kT_ref, v_ref, ga_ref, mb_ref, wm_scr, ysg_scr):
    pair_rows = 2 * SG_BLOCK

    @pl.when(pl.program_id(0) == 0)
    def _():
        t = lax.broadcasted_iota(jnp.int32, (pair_rows, SG_BLOCK), 0) % SG_BLOCK
        s = lax.broadcasted_iota(jnp.int32, (pair_rows, SG_BLOCK), 1)
        mask = (s // CHUNK) <= (t // CHUNK)
        for j in range(SG_GROUPS // 2):
            wm_scr[j] = jnp.where(mask, sgw_ref[j], 0.0).astype(BF16)

    hb = _rmsnorm(x_ref[...], g_ref[...]).astype(BF16)
    vs_raw = _dot(hb, wvs_ref[...])
    u_raw = _dot(hb, wu_ref[...])
    q_ref[...] = (_dot(hb, wq_ref[...]) * (ATT_HEAD_DIM ** -0.5 * LOG2E)).astype(BF16)
    vs = jax.nn.gelu(vs_raw)

    def group_mean(t):
        tb = t.astype(BF16)
        w = avg_ref.shape[0]
        return jnp.concatenate([_dot(tb[:, c:c + w], avg_ref[...])
                                for c in range(0, SG_WIDTH, w)], axis=1)

    mean = group_mean(vs)
    kT_ref[...] = _dot(hb, wk_ref[...]).T.astype(BF16)
    dev = vs - mean
    var = group_mean(dev * dev)
    v_ref[...] = _dot(hb, wv_ref[...]).astype(BF16)
    ga_raw = _dot(hb, wga_ref[...])
    vln = (dev * lax.rsqrt(var + EPS) * lng_ref[...] + lnb_ref[...]).astype(BF16)
    u = jax.nn.gelu(u_raw)

    lane = lax.broadcasted_iota(jnp.int32, (SG_BLOCK, LANES), 1)
    for n in range(x_ref.shape[0] // SG_BLOCK):
        rows = slice(n * SG_BLOCK, (n + 1) * SG_BLOCK)
        for j in range(SG_GROUPS // 2):
            cols = slice(j * LANES, (j + 1) * LANES)
            r = _dot(wm_scr[j], vln[rows, cols])
            sv = jnp.where(lane < SG_GROUP_DIM, r[:SG_BLOCK], r[SG_BLOCK:]) + sgb_ref[:, cols]
            ysg_scr[rows, cols] = (u[rows, cols] * sv).astype(BF16)

    gate_b = jax.nn.sigmoid(_dot(hb, wgb_ref[...]))
    mb_ref[...] = (gate_b * _dot(ysg_scr[...], wbsg_ref[...])).astype(BF16)
    ga_ref[...] = jax.nn.sigmoid(ga_raw).astype(BF16)


def _inproj(x, g, wq, wk, wv, wu, wvs, wga, wgb, avg, lng, lnb, sgw, sgb, wbsg):
    n, d = x.shape
    tm = INPROJ_TILE
    assert tm % SG_BLOCK == 0
    tile = lambda w: pl.BlockSpec((tm, w), lambda i: (i, 0))
    return pl.pallas_call(
        _inproj_kernel,
        grid=(n // tm,),
        out_shape=(jax.ShapeDtypeStruct((n, ATT_WIDTH), BF16),
                   jax.ShapeDtypeStruct((ATT_WIDTH, n), BF16),
                   jax.ShapeDtypeStruct((n, ATT_WIDTH), BF16),
                   jax.ShapeDtypeStruct((n, d), BF16),
                   jax.ShapeDtypeStruct((n, d), BF16)),
        in_specs=[tile(d), _const_spec((1, d)),
                  _const_spec(wq.shape), _const_spec(wk.shape), _const_spec(wv.shape),
                  _const_spec(wu.shape), _const_spec(wvs.shape), _const_spec(wga.shape),
                  _const_spec(wgb.shape), _const_spec(avg.shape), _const_spec(lng.shape),
                  _const_spec(lnb.shape), _const_spec(sgw.shape), _const_spec(sgb.shape),
                  _const_spec(wbsg.shape)],
        out_specs=(tile(ATT_WIDTH), pl.BlockSpec((ATT_WIDTH, tm), lambda i: (0, i)),
                   tile(ATT_WIDTH), tile(d), tile(d)),
        scratch_shapes=[pltpu.VMEM((SG_GROUPS // 2, 2 * SG_BLOCK, SG_BLOCK), BF16),
                        pltpu.VMEM((tm, SG_WIDTH), BF16)],
        compiler_params=_params(),
        name="inproj_sgu",
    )(x, g, wq, wk, wv, wu, wvs, wga, wgb, avg, lng, lnb, sgw, sgb, wbsg)


def _attn_kernel(tiles_per_seq, x_ref, q_ref, kTp_ref, kTc_ref, vp_ref, vc_ref, ga_ref, mb_ref,
                 bias_ref, wbatt_ref, wout_ref, gx_ref, wxq_ref, kmT_ref, vm_ref, wxo_ref,
                 o_ref, kT_scr, v_scr, yatt_scr, xo_scr):
    tm = x_ref.shape[0]
    d = x_ref.shape[1]
    first_tile = (pl.program_id(0) % tiles_per_seq) == 0

    kcat = jnp.concatenate([kTp_ref[...], kTc_ref[...]], axis=1)
    low_rows = (lax.broadcasted_iota(jnp.int32, kcat.shape, 0) % LANES) < ATT_HEAD_DIM
    kT_scr[0] = jnp.where(low_rows, kcat, jnp.zeros_like(kcat))
    kT_scr[1] = jnp.where(low_rows, jnp.zeros_like(kcat), kcat)
    vcat = jnp.concatenate([vp_ref[...], vc_ref[...]], axis=0)
    low_half = (lax.broadcasted_iota(jnp.int32, vcat.shape, 1) % LANES) < ATT_HEAD_DIM
    v_scr[0] = jnp.where(low_half, vcat, jnp.ones_like(vcat))
    v_scr[1] = jnp.where(low_half, jnp.ones_like(vcat), vcat)

    lane = lax.broadcasted_iota(jnp.int32, (Q_SUB, LANES), 1)
    n_blocks = WINDOW // LANES

    def window(j, h):
        q0 = j * Q_SUB
        cols = slice((h // 2) * LANES, (h // 2 + 1) * LANES)
        return q0, q0, cols

    def scores(j, p):
        q0, w0, cols = window(j, 2 * p)
        qp = q_ref[q0:q0 + Q_SUB, cols]
        kw = jnp.concatenate([kT_scr[0, cols, w0:w0 + WINDOW], kT_scr[1, cols, w0:w0 + WINDOW]],
                             axis=1)
        s = _dot(qp, kw)
        heads = []
        for hh in range(2):
            h = 2 * p + hh
            blocks = []
            for b in range(n_blocks):
                before_start = (b + 1) * LANES <= PREV_KEYS - q0
                entry = jnp.where(first_tile, ATT_HEADS, h) if before_start else h
                blk = slice(b * LANES, (b + 1) * LANES)
                sblk = slice(hh * WINDOW + b * LANES, hh * WINDOW + (b + 1) * LANES)
                blocks.append(s[:, sblk] + bias_ref[entry, :, blk])
            heads.append(jnp.concatenate(blocks, axis=1))
        return heads

    def attend(j, h, s):
        _, w0, cols = window(j, h)
        p = jnp.exp2(s - jnp.max(s, axis=-1, keepdims=True)).astype(BF16)
        o = _dot(p, v_scr[h % 2, w0:w0 + WINDOW, cols])
        return o / pltpu.roll(o, ATT_HEAD_DIM, axis=1)

    steps = [(j, p) for j in range(tm // Q_SUB) for p in range(ATT_HEADS // 2)]
    pending = [scores(*st) for st in steps[:SCORE_LOOKAHEAD]]
    for k, (j, p) in enumerate(steps):
        if k + SCORE_LOOKAHEAD < len(steps):
            pending.append(scores(*steps[k + SCORE_LOOKAHEAD]))
        s_even, s_odd = pending.pop(0)
        o_even = attend(j, 2 * p, s_even)
        o_odd = attend(j, 2 * p + 1, s_odd)
        q0, _, cols = window(j, 2 * p)
        y = jnp.where(lane < ATT_HEAD_DIM, o_even, o_odd)
        yatt_scr[q0:q0 + Q_SUB, cols] = y.astype(BF16)

    groups = [slice(r, r + TAIL_ROWS) for r in range(0, tm, TAIL_ROWS)]
    xd = d // XATT_HEADS
    heads = [slice(h * xd, (h + 1) * xd) for h in range(XATT_HEADS)]
    att = [_dot(yatt_scr[g, :], wbatt_ref[...]) for g in groups]
    x1 = []
    for g, a in zip(groups, att):
        merged = ga_ref[g, :].astype(F32) * a + mb_ref[g, :].astype(F32)
        x1.append(x_ref[g, :] + _dot(merged.astype(BF16), wout_ref[...]))
    qx = [(_dot(_rmsnorm(v, gx_ref[...]).astype(BF16), wxq_ref[...])
           * (xd ** -0.5 * LOG2E)).astype(BF16) for v in x1]
    xs = [[_dot(q[:, cols], kmT_ref[0, cols, :]) for q in qx] for cols in heads]
    for gi, (g, v) in enumerate(zip(groups, x1)):
        for cols, per_group in zip(heads, xs):
            s = per_group[gi]
            e = jnp.exp2(s - jnp.max(s, axis=-1, keepdims=True))
            den = jnp.sum(e, axis=-1, keepdims=True)
            xo_scr[g, cols] = (_dot(e.astype(BF16), vm_ref[0, :, cols]) / den).astype(BF16)
        o_ref[g, :] = v + _dot(xo_scr[g, :], wxo_ref[...])


def _attn(x, q, kT, v, ga, mb, bias, wbatt, wout, gx, wxq, kmT, vm, wxo, seq):
    n, d = x.shape
    tm = TOKEN_TILE
    tps = seq // tm
    mem_len = vm.shape[1]
    tile = lambda w: pl.BlockSpec((tm, w), lambda i: (i, 0))
    prev = lambda i: jnp.maximum(i * (tm // PREV_KEYS) - 1, 0)
    return pl.pallas_call(
        functools.partial(_attn_kernel, tps),
        grid=(n // tm,),
        out_shape=jax.ShapeDtypeStruct((n, d), F32),
        in_specs=[tile(d), tile(ATT_WIDTH),
                  pl.BlockSpec((ATT_WIDTH, PREV_KEYS), lambda i: (0, prev(i))),
                  pl.BlockSpec((ATT_WIDTH, tm), lambda i: (0, i)),
                  pl.BlockSpec((PREV_KEYS, ATT_WIDTH), lambda i: (prev(i), 0)),
                  tile(ATT_WIDTH), tile(d), tile(d),
                  _const_spec(bias.shape), _const_spec(wbatt.shape), _const_spec(wout.shape),
                  _const_spec((1, d)), _const_spec(wxq.shape),
                  pl.BlockSpec((1, d, mem_len), lambda i: (i // tps, 0, 0)),
                  pl.BlockSpec((1, mem_len, d), lambda i: (i // tps, 0, 0)),
                  _const_spec(wxo.shape)],
        out_specs=tile(d),
        scratch_shapes=[pltpu.VMEM((2, ATT_WIDTH, PREV_KEYS + tm), BF16),
                        pltpu.VMEM((2, PREV_KEYS + tm, ATT_WIDTH), BF16),
                        pltpu.VMEM((tm, ATT_WIDTH), BF16),
                        pltpu.VMEM((tm, d), BF16)],
        compiler_params=_params(),
        name="attn_merge_xattn",
    )(x, q, kT, kT, v, v, ga, mb, bias, wbatt, wout, gx, wxq, kmT, vm, wxo)


def _ffn_kernel(x_ref, g_ref, wfi_ref, wfo_ref, gfin_ref, o_ref, a_scr):
    d_ff = wfo_ref.shape[0]
    subs = [slice(r, r + FFN_SUB) for r in range(0, x_ref.shape[0], FFN_SUB)]
    def finish(rows):
        x3 = x_ref[rows, :] + _dot(a_scr[rows, :], wfo_ref[...])
        o_ref[rows, :] = _rmsnorm(x3, gfin_ref[...])

    hb_next = _rmsnorm(x_ref[subs[0], :], g_ref[...]).astype(BF16)
    for r, rows in enumerate(subs):
        hb = hb_next
        for c in range(d_ff // FF_CHUNK):
            gate = _dot(hb, wfi_ref[:, c * FF_CHUNK:(c + 1) * FF_CHUNK])
            up = _dot(hb, wfi_ref[:, d_ff + c * FF_CHUNK:d_ff + (c + 1) * FF_CHUNK])
            a_scr[rows, c * FF_CHUNK:(c + 1) * FF_CHUNK] = (jax.nn.silu(gate) * up).astype(BF16)
            if c == 0 and r + 1 < len(subs):
                hb_next = _rmsnorm(x_ref[subs[r + 1], :], g_ref[...]).astype(BF16)
            if c == 0 and r > 0:
                finish(subs[r - 1])
    finish(subs[-1])


def _ffn(x, g, wfi, wfo, gfin):
    n, d = x.shape
    tm = FFN_TILE
    d_ff = wfo.shape[0]
    assert d_ff % FF_CHUNK == 0
    tile = pl.BlockSpec((tm, d), lambda i: (i, 0))
    return pl.pallas_call(
        _ffn_kernel,
        grid=(n // tm,),
        out_shape=jax.ShapeDtypeStruct((n, d), F32),
        in_specs=[tile, _const_spec((1, d)), _const_spec(wfi.shape), _const_spec(wfo.shape),
                  _const_spec((1, d))],
        out_specs=tile,
        scratch_shapes=[pltpu.VMEM((tm, d_ff), BF16)],
        compiler_params=_params(),
        name="ffn_final_norm",
    )(x, g, wfi, wfo, gfin)


def kernel(x, mem, norm_mix_g, w_in, rel_bias, sg_ln_g, sg_ln_b, sg_w, sg_b, w_branch_att,
           w_branch_sg, w_out, norm_xattn_g, norm_mem_g, w_xq, w_xkv, w_xo, norm_ffn_g,
           w_ffn_in, w_ffn_out, norm_final_g):
    b, s, d = x.shape
    depth = w_in.shape[0]
    assert s % TOKEN_TILE == 0 and TOKEN_TILE % PREV_KEYS == 0 and TOKEN_TILE % Q_SUB == 0
    assert (b * s) % FFN_TILE == 0 and (b * s) % INPROJ_TILE == 0
    row = lambda a: a.reshape(1, -1).astype(F32)
    grp = np.arange(MXU_WIDTH) // SG_GROUP_DIM
    avg = jnp.asarray((grp[:, None] == grp[None, :]) / SG_GROUP_DIM, BF16)

    xf = x.reshape(b * s, d)
    for l in range(depth):
        bounds = np.cumsum([0, ATT_WIDTH, ATT_WIDTH, ATT_WIDTH, SG_WIDTH, SG_WIDTH, d, d])
        wq, wk, wv, wu, wvs, wga, wgb = (
            w_in[l, :, int(lo):int(hi)].astype(BF16) for lo, hi in zip(bounds[:-1], bounds[1:]))
        sgw = sg_w[l].reshape(SG_GROUPS // 2, 2 * SG_BLOCK, SG_BLOCK)
        sgb = jnp.repeat(sg_b[l].T, SG_GROUP_DIM, axis=1)

        bias = _bias_table(rel_bias[l])
        kmT, vm = _mem_kv(mem, row(norm_mem_g[l]), w_xkv[l, :, :d].T.astype(BF16),
                          w_xkv[l, :, d:].astype(BF16))
        q, kT, v, ga, mb = _inproj(
            xf, row(norm_mix_g[l]), wq, wk, wv, wu, wvs, wga, wgb, avg,
            row(sg_ln_g[l]), row(sg_ln_b[l]), sgw, sgb, w_branch_sg[l].astype(BF16))
        xf = _attn(xf, q, kT, v, ga, mb, bias, w_branch_att[l].astype(BF16),
                   w_out[l].astype(BF16), row(norm_xattn_g[l]), w_xq[l].astype(BF16),
                   kmT, vm, w_xo[l].astype(BF16), s)
        assert depth == 1
        xf = _ffn(xf, row(norm_ffn_g[l]), w_ffn_in[l].astype(BF16), w_ffn_out[l].astype(BF16),
                  row(norm_final_g))
    return xf.reshape(b, s, d)
```

```python
import functools

import jax
import jax.numpy as jnp
import numpy as np
from jax import lax
from jax.experimental import pallas as pl
from jax.experimental.pallas import tpu as pltpu

CHUNK = 64
N_PREV_CHUNKS = 8
ATT_HEADS = 8
ATT_HEAD_DIM = 64
ATT_WIDTH = ATT_HEADS * ATT_HEAD_DIM
REL_CLIP = 128
SG_BLOCK = 128
SG_GROUPS = 8
SG_GROUP_DIM = 64
SG_WIDTH = SG_GROUPS * SG_GROUP_DIM
XATT_HEADS = 4
EPS = 1e-6
NEG_INF = -1e30
LOG2E = 1.4426950408889634

LANES = 128
MXU_WIDTH = 256
VMEM_LIMIT_BYTES = 62 * 1024 * 1024

TOKEN_TILE = 1024
TAIL_ROWS = 512
INPROJ_TILE = 1024
FFN_TILE = 1024
FFN_SUB = 512
Q_SUB = 2 * CHUNK
PREV_KEYS = N_PREV_CHUNKS * CHUNK
WINDOW = PREV_KEYS + Q_SUB
FF_CHUNK = 256
SCORE_LOOKAHEAD = 2

F32 = jnp.float32
BF16 = jnp.bfloat16
_NT = (((1,), (1,)), ((), ()))


def _dot(a, b):
    return jnp.dot(a, b, preferred_element_type=F32)


def _dot_nt(a, b):
    return lax.dot_general(a, b, _NT, preferred_element_type=F32)


def _rmsnorm(x, g):
    return x * lax.rsqrt(jnp.mean(x * x, axis=-1, keepdims=True) + EPS) * g


def _const_spec(shape):
    zeros = (0,) * len(shape)
    return pl.BlockSpec(shape, lambda *_: zeros, pipeline_mode=pl.Buffered(1))


def _params():
    return pltpu.CompilerParams(dimension_semantics=("arbitrary",),
                                vmem_limit_bytes=VMEM_LIMIT_BYTES)


def _bias_kernel(rb_ref, o_ref):
    near_cols = Q_SUB + REL_CLIP
    far_cols = WINDOW - near_cols
    width = near_cols + Q_SUB
    c = lax.broadcasted_iota(jnp.int32, (1, width), 1)
    idx = jnp.clip(near_cols - c, -REL_CLIP, REL_CLIP) + REL_CLIP
    diag = jnp.zeros((ATT_HEADS, width), F32)
    for j in range(2 * REL_CLIP + 1):
        diag = jnp.where(idx == j, rb_ref[:, j:j + 1], diag)
    diag = diag * LOG2E

    row = lax.broadcasted_iota(jnp.int32, (Q_SUB, width), 0)
    qc = lax.broadcasted_iota(jnp.int32, (Q_SUB, WINDOW), 0) // CHUNK
    kc = lax.broadcasted_iota(jnp.int32, (Q_SUB, WINDOW), 1) // CHUNK
    valid = (kc >= qc) & (kc <= qc + N_PREV_CHUNKS)
    for h in range(ATT_HEADS):
        t = jnp.broadcast_to(diag[h:h + 1, :], (Q_SUB, width))
        for bit in range(Q_SUB.bit_length() - 1):
            t = jnp.where(((row >> bit) & 1) == 1, pltpu.roll(t, 1 << bit, axis=1), t)
        far = jnp.broadcast_to(diag[h:h + 1, 0:1], (Q_SUB, far_cols))
        o_ref[h] = jnp.where(valid, jnp.concatenate([far, t[:, Q_SUB:]], axis=1), NEG_INF)
    o_ref[ATT_HEADS] = jnp.full((Q_SUB, WINDOW), NEG_INF, F32)


def _bias_table(rel_bias):
    return pl.pallas_call(
        _bias_kernel,
        out_shape=jax.ShapeDtypeStruct((ATT_HEADS + 1, Q_SUB, WINDOW), F32),
        in_specs=[pl.BlockSpec(memory_space=pltpu.VMEM)],
        out_specs=pl.BlockSpec(memory_space=pltpu.VMEM),
        name="bias_table",
    )(rel_bias)


def _memkv_kernel(mem_ref, g_ref, wkT_ref, wv_ref, kT_ref, v_ref):
    mn = _rmsnorm(mem_ref[0], g_ref[...]).astype(BF16)
    kT_ref[0] = _dot_nt(wkT_ref[...], mn).astype(BF16)
    v_ref[0] = _dot(mn, wv_ref[...]).astype(BF16)


def _mem_kv(mem, g, wkT, wv):
    b, m, d = mem.shape
    return pl.pallas_call(
        _memkv_kernel,
        grid=(b,),
        out_shape=(jax.ShapeDtypeStruct((b, d, m), BF16), jax.ShapeDtypeStruct((b, m, d), BF16)),
        in_specs=[pl.BlockSpec((1, m, d), lambda i: (i, 0, 0)),
                  _const_spec((1, d)), _const_spec((d, d)), _const_spec((d, d))],
        out_specs=(pl.BlockSpec((1, d, m), lambda i: (i, 0, 0)),
                   pl.BlockSpec((1, m, d), lambda i: (i, 0, 0))),
        compiler_params=_params(),
        name="mem_kv",
    )(mem, g, wkT, wv)


def _inproj_kernel(x_ref, g_ref, wq_ref, wk_ref, wv_ref, wu_ref, wvs_ref, wga_ref, wgb_ref,
                   avg_ref, lng_ref, lnb_ref, sgw_ref, sgb_ref, wbsg_ref,
                   q_ref, kT_ref, v_ref, ga_ref, mb_ref, wm_scr, ysg_scr):
    pair_rows = 2 * SG_BLOCK

    @pl.when(pl.program_id(0) == 0)
    def _():
        t = lax.broadcasted_iota(jnp.int32, (pair_rows, SG_BLOCK), 0) % SG_BLOCK
        s = lax.broadcasted_iota(jnp.int32, (pair_rows, SG_BLOCK), 1)
        mask = (s // CHUNK) <= (t // CHUNK)
        for j in range(SG_GROUPS // 2):
            wm_scr[j] = jnp.where(mask, sgw_ref[j], 0.0).astype(BF16)

    hb = _rmsnorm(x_ref[...], g_ref[...]).astype(BF16)
    vs_raw = _dot(hb, wvs_ref[...])
    u_raw = _dot(hb, wu_ref[...])
    q_ref[...] = (_dot(hb, wq_ref[...]) * (ATT_HEAD_DIM ** -0.5 * LOG2E)).astype(BF16)
    vs = jax.nn.gelu(vs_raw)

    def group_mean(t):
        tb = t.astype(BF16)
        w = avg_ref.shape[0]
        return jnp.concatenate([_dot(tb[:, c:c + w], avg_ref[...])
                                for c in range(0, SG_WIDTH, w)], axis=1)

    mean = group_mean(vs)
    kT_ref[...] = _dot(hb, wk_ref[...]).T.astype(BF16)
    dev = vs - mean
    var = group_mean(dev * dev)
    v_ref[...] = _dot(hb, wv_ref[...]).astype(BF16)
    ga_raw = _dot(hb, wga_ref[...])
    vln = (dev * lax.rsqrt(var + EPS) * lng_ref[...] + lnb_ref[...]).astype(BF16)
    u = jax.nn.gelu(u_raw)

    lane = lax.broadcasted_iota(jnp.int32, (SG_BLOCK, LANES), 1)
    for n in range(x_ref.shape[0] // SG_BLOCK):
        rows = slice(n * SG_BLOCK, (n + 1) * SG_BLOCK)
        for j in range(SG_GROUPS // 2):
            cols = slice(j * LANES, (j + 1) * LANES)
            r = _dot(wm_scr[j], vln[rows, cols])
            sv = jnp.where(lane < SG_GROUP_DIM, r[:SG_BLOCK], r[SG_BLOCK:]) + sgb_ref[:, cols]
            ysg_scr[rows, cols] = (u[rows, cols] * sv).astype(BF16)

    gate_b = jax.nn.sigmoid(_dot(hb, wgb_ref[...]))
    mb_ref[...] = (gate_b * _dot(ysg_scr[...], wbsg_ref[...])).astype(BF16)
    ga_ref[...] = jax.nn.sigmoid(ga_raw).astype(BF16)


def _inproj(x, g, wq, wk, wv, wu, wvs, wga, wgb, avg, lng, lnb, sgw, sgb, wbsg):
    n, d = x.shape
    tm = INPROJ_TILE
    assert tm % SG_BLOCK == 0
    tile = lambda w: pl.BlockSpec((tm, w), lambda i: (i, 0))
    return pl.pallas_call(
        _inproj_kernel,
        grid=(n // tm,),
        out_shape=(jax.ShapeDtypeStruct((n, ATT_WIDTH), BF16),
                   jax.ShapeDtypeStruct((ATT_WIDTH, n), BF16),
                   jax.ShapeDtypeStruct((n, ATT_WIDTH), BF16),
                   jax.ShapeDtypeStruct((n, d), BF16),
                   jax.ShapeDtypeStruct((n, d), BF16)),
        in_specs=[tile(d), _const_spec((1, d)),
                  _const_spec(wq.shape), _const_spec(wk.shape), _const_spec(wv.shape),
                  _const_spec(wu.shape), _const_spec(wvs.shape), _const_spec(wga.shape),
                  _const_spec(wgb.shape), _const_spec(avg.shape), _const_spec(lng.shape),
                  _const_spec(lnb.shape), _const_spec(sgw.shape), _const_spec(sgb.shape),
                  _const_spec(wbsg.shape)],
        out_specs=(tile(ATT_WIDTH), pl.BlockSpec((ATT_WIDTH, tm), lambda i: (0, i)),
                   tile(ATT_WIDTH), tile(d), tile(d)),
        scratch_shapes=[pltpu.VMEM((SG_GROUPS // 2, 2 * SG_BLOCK, SG_BLOCK), BF16),
                        pltpu.VMEM((tm, SG_WIDTH), BF16)],
        compiler_params=_params(),
        name="inproj_sgu",
    )(x, g, wq, wk, wv, wu, wvs, wga, wgb, avg, lng, lnb, sgw, sgb, wbsg)


def _attn_kernel(tiles_per_seq, x_ref, q_ref, kTp_ref, kTc_ref, vp_ref, vc_ref, ga_ref, mb_ref,
                 bias_ref, wbatt_ref, wout_ref, gx_ref, wxq_ref, kmT_ref, vm_ref, wxo_ref,
                 o_ref, kT_scr, v_scr, yatt_scr, xo_scr):
    tm = x_ref.shape[0]
    d = x_ref.shape[1]
    first_tile = (pl.program_id(0) % tiles_per_seq) == 0

    kcat = jnp.concatenate([kTp_ref[...], kTc_ref[...]], axis=1)
    low_rows = (lax.broadcasted_iota(jnp.int32, kcat.shape, 0) % LANES) < ATT_HEAD_DIM
    kT_scr[0] = jnp.where(low_rows, kcat, jnp.zeros_like(kcat))
    kT_scr[1] = jnp.where(low_rows, jnp.zeros_like(kcat), kcat)
    vcat = jnp.concatenate([vp_ref[...], vc_ref[...]], axis=0)
    low_half = (lax.broadcasted_iota(jnp.int32, vcat.shape, 1) % LANES) < ATT_HEAD_DIM
    v_scr[0] = jnp.where(low_half, vcat, jnp.ones_like(vcat))
    v_scr[1] = jnp.where(low_half, jnp.ones_like(vcat), vcat)

    lane = lax.broadcasted_iota(jnp.int32, (Q_SUB, LANES), 1)
    n_blocks = WINDOW // LANES

    def window(j, h):
        q0 = j * Q_SUB
        cols = slice((h // 2) * LANES, (h // 2 + 1) * LANES)
        return q0, q0, cols

    def scores(j, p):
        q0, w0, cols = window(j, 2 * p)
        qp = q_ref[q0:q0 + Q_SUB, cols]
        kw = jnp.concatenate([kT_scr[0, cols, w0:w0 + WINDOW], kT_scr[1, cols, w0:w0 + WINDOW]],
                             axis=1)
        s = _dot(qp, kw)
        heads = []
        for hh in range(2):
            h = 2 * p + hh
            blocks = []
            for b in range(n_blocks):
                before_start = (b + 1) * LANES <= PREV_KEYS - q0
                entry = jnp.where(first_tile, ATT_HEADS, h) if before_start else h
                blk = slice(b * LANES, (b + 1) * LANES)
                sblk = slice(hh * WINDOW + b * LANES, hh * WINDOW + (b + 1) * LANES)
                blocks.append(s[:, sblk] + bias_ref[entry, :, blk])
            heads.append(jnp.concatenate(blocks, axis=1))
        return heads

    def attend(j, h, s):
        _, w0, cols = window(j, h)
        p = jnp.exp2(s - jnp.max(s, axis=-1, keepdims=True)).astype(BF16)
        o = _dot(p, v_scr[h % 2, w0:w0 + WINDOW, cols])
        return o / pltpu.roll(o, ATT_HEAD_DIM, axis=1)

    steps = [(j, p) for j in range(tm // Q_SUB) for p in range(ATT_HEADS // 2)]
    pending = [scores(*st) for st in steps[:SCORE_LOOKAHEAD]]
    for k, (j, p) in enumerate(steps):
        if k + SCORE_LOOKAHEAD < len(steps):
            pending.append(scores(*steps[k + SCORE_LOOKAHEAD]))
        s_even, s_odd = pending.pop(0)
        o_even = attend(j, 2 * p, s_even)
        o_odd = attend(j, 2 * p + 1, s_odd)
        q0, _, cols = window(j, 2 * p)
        y = jnp.where(lane < ATT_HEAD_DIM, o_even, o_odd)
        yatt_scr[q0:q0 + Q_SUB, cols] = y.astype(BF16)

    groups = [slice(r, r + TAIL_ROWS) for r in range(0, tm, TAIL_ROWS)]
    xd = d // XATT_HEADS
    heads = [slice(h * xd, (h + 1) * xd) for h in range(XATT_HEADS)]
    att = [_dot(yatt_scr[g, :], wbatt_ref[...]) for g in groups]
    x1 = []
    for g, a in zip(groups, att):
        merged = ga_ref[g, :].astype(F32) * a + mb_ref[g, :].astype(F32)
        x1.append(x_ref[g, :] + _dot(merged.astype(BF16), wout_ref[...]))
    qx = [(_dot(_rmsnorm(v, gx_ref[...]).astype(BF16), wxq_ref[...])
           * (xd ** -0.5 * LOG2E)).astype(BF16) for v in x1]
    xs = [[_dot(q[:, cols], kmT_ref[0, cols, :]) for q in qx] for cols in heads]
    for gi, (g, v) in enumerate(zip(groups, x1)):
        for cols, per_group in zip(heads, xs):
            s = per_group[gi]
            e = jnp.exp2(s - jnp.max(s, axis=-1, keepdims=True))
            den = jnp.sum(e, axis=-1, keepdims=True)
            xo_scr[g, cols] = (_dot(e.astype(BF16), vm_ref[0, :, cols]) / den).astype(BF16)
        o_ref[g, :] = v + _dot(xo_scr[g, :], wxo_ref[...])


def _attn(x, q, kT, v, ga, mb, bias, wbatt, wout, gx, wxq, kmT, vm, wxo, seq):
    n, d = x.shape
    tm = TOKEN_TILE
    tps = seq // tm
    mem_len = vm.shape[1]
    tile = lambda w: pl.BlockSpec((tm, w), lambda i: (i, 0))
    prev = lambda i: jnp.maximum(i * (tm // PREV_KEYS) - 1, 0)
    return pl.pallas_call(
        functools.partial(_attn_kernel, tps),
        grid=(n // tm,),
        out_shape=jax.ShapeDtypeStruct((n, d), F32),
        in_specs=[tile(d), tile(ATT_WIDTH),
                  pl.BlockSpec((ATT_WIDTH, PREV_KEYS), lambda i: (0, prev(i))),
                  pl.BlockSpec((ATT_WIDTH, tm), lambda i: (0, i)),
                  pl.BlockSpec((PREV_KEYS, ATT_WIDTH), lambda i: (prev(i), 0)),
                  tile(ATT_WIDTH), tile(d), tile(d),
                  _const_spec(bias.shape), _const_spec(wbatt.shape), _const_spec(wout.shape),
                  _const_spec((1, d)), _const_spec(wxq.shape),
                  pl.BlockSpec((1, d, mem_len), lambda i: (i // tps, 0, 0)),
                  pl.BlockSpec((1, mem_len, d), lambda i: (i // tps, 0, 0)),
                  _const_spec(wxo.shape)],
        out_specs=tile(d),
        scratch_shapes=[pltpu.VMEM((2, ATT_WIDTH, PREV_KEYS + tm), BF16),
                        pltpu.VMEM((2, PREV_KEYS + tm, ATT_WIDTH), BF16),
                        pltpu.VMEM((tm, ATT_WIDTH), BF16),
                        pltpu.VMEM((tm, d), BF16)],
        compiler_params=_params(),
        name="attn_merge_xattn",
    )(x, q, kT, kT, v, v, ga, mb, bias, wbatt, wout, gx, wxq, kmT, vm, wxo)


def _ffn_kernel(x_ref, g_ref, wfi_ref, wfo_ref, gfin_ref, o_ref, a_scr):
    d_ff = wfo_ref.shape[0]
    subs = [slice(r, r + FFN_SUB) for r in range(0, x_ref.shape[0], FFN_SUB)]
    def finish(rows):
        x3 = x_ref[rows, :] + _dot(a_scr[rows, :], wfo_ref[...])
        o_ref[rows, :] = _rmsnorm(x3, gfin_ref[...])

    hb_next = _rmsnorm(x_ref[subs[0], :], g_ref[...]).astype(BF16)
    for r, rows in enumerate(subs):
        hb = hb_next
        for c in range(d_ff // FF_CHUNK):
            gate = _dot(hb, wfi_ref[:, c * FF_CHUNK:(c + 1) * FF_CHUNK])
            up = _dot(hb, wfi_ref[:, d_ff + c * FF_CHUNK:d_ff + (c + 1) * FF_CHUNK])
            a_scr[rows, c * FF_CHUNK:(c + 1) * FF_CHUNK] = (jax.nn.silu(gate) * up).astype(BF16)
            if c == 0 and r + 1 < len(subs):
                hb_next = _rmsnorm(x_ref[subs[r + 1], :], g_ref[...]).astype(BF16)
            if c == 0 and r > 0:
                finish(subs[r - 1])
    finish(subs[-1])


def _ffn(x, g, wfi, wfo, gfin):
    n, d = x.shape
    tm = FFN_TILE
    d_ff = wfo.shape[0]
    assert d_ff % FF_CHUNK == 0
    tile = pl.BlockSpec((tm, d), lambda i: (i, 0))
    return pl.pallas_call(
        _ffn_kernel,
        grid=(n // tm,),
        out_shape=jax.ShapeDtypeStruct((n, d), F32),
        in_specs=[tile, _const_spec((1, d)), _const_spec(wfi.shape), _const_spec(wfo.shape),
                  _const_spec((1, d))],
        out_specs=tile,
        scratch_shapes=[pltpu.VMEM((tm, d_ff), BF16)],
        compiler_params=pltpu.CompilerParams(dimension_semantics=("parallel",),
                                             vmem_limit_bytes=VMEM_LIMIT_BYTES),
        name="ffn_final_norm",
    )(x, g, wfi, wfo, gfin)


def kernel(x, mem, norm_mix_g, w_in, rel_bias, sg_ln_g, sg_ln_b, sg_w, sg_b, w_branch_att,
           w_branch_sg, w_out, norm_xattn_g, norm_mem_g, w_xq, w_xkv, w_xo, norm_ffn_g,
           w_ffn_in, w_ffn_out, norm_final_g):
    b, s, d = x.shape
    depth = w_in.shape[0]
    assert s % TOKEN_TILE == 0 and TOKEN_TILE % PREV_KEYS == 0 and TOKEN_TILE % Q_SUB == 0
    assert (b * s) % FFN_TILE == 0 and (b * s) % INPROJ_TILE == 0
    row = lambda a: a.reshape(1, -1).astype(F32)
    grp = np.arange(MXU_WIDTH) // SG_GROUP_DIM
    avg = jnp.asarray((grp[:, None] == grp[None, :]) / SG_GROUP_DIM, BF16)

    xf = x.reshape(b * s, d)
    for l in range(depth):
        bounds = np.cumsum([0, ATT_WIDTH, ATT_WIDTH, ATT_WIDTH, SG_WIDTH, SG_WIDTH, d, d])
        wq, wk, wv, wu, wvs, wga, wgb = (
            w_in[l, :, int(lo):int(hi)].astype(BF16) for lo, hi in zip(bounds[:-1], bounds[1:]))
        sgw = sg_w[l].reshape(SG_GROUPS // 2, 2 * SG_BLOCK, SG_BLOCK)
        sgb = jnp.repeat(sg_b[l].T, SG_GROUP_DIM, axis=1)

        bias = _bias_table(rel_bias[l])
        kmT, vm = _mem_kv(mem, row(norm_mem_g[l]), w_xkv[l, :, :d].T.astype(BF16),
                          w_xkv[l, :, d:].astype(BF16))
        q, kT, v, ga, mb = _inproj(
            xf, row(norm_mix_g[l]), wq, wk, wv, wu, wvs, wga, wgb, avg,
            row(sg_ln_g[l]), row(sg_ln_b[l]), sgw, sgb, w_branch_sg[l].astype(BF16))
        xf = _attn(xf, q, kT, v, ga, mb, bias, w_branch_att[l].astype(BF16),
                   w_out[l].astype(BF16), row(norm_xattn_g[l]), w_xq[l].astype(BF16),
                   kmT, vm, w_xo[l].astype(BF16), s)
        assert depth == 1
        xf = _ffn(xf, row(norm_ffn_g[l]), w_ffn_in[l].astype(BF16), w_ffn_out[l].astype(BF16),
                  row(norm_final_g))
    return xf.reshape(b, s, d)
```
